```python
import math
import jax, jax.numpy as jnp
from jax import lax
import numpy as np

D_MODEL = 4096
BATCH = 1
SEQ = 8192
DEPTH = 4

D_CONV = D_MODEL // 2
CONV_WIDTH = 31
N_HEADS = D_MODEL // 128
QK_NOPE_DIM = 128
QK_ROPE_DIM = 64
QK_DIM = QK_NOPE_DIM + QK_ROPE_DIM
V_DIM = 128
Q_LORA = 1536
KV_LORA = 512
D_MLA = N_HEADS * V_DIM
ROPE_THETA = 10000.0
Q_BLOCK = 128
D_LRU = D_MODEL // 2
LRU_BLOCKS = 16
LRU_BLOCK_DIM = D_LRU // LRU_BLOCKS
LRU_CONV_WIDTH = 4
LRU_C = 8.0
N_BRANCH = 3
EPS = 1e-6

IN_SPLITS = (2 * D_CONV, D_CONV,
             Q_LORA, KV_LORA, QK_ROPE_DIM, D_MLA,
             D_LRU, D_LRU,
             N_BRANCH * D_MODEL)
D_IN = sum(IN_SPLITS)

kernel_name = 'hybrid_gated_conv_mla_rglru'


def rms_norm(x, g):
    xf = x.astype(jnp.float32)
    y = xf * lax.rsqrt(jnp.mean(xf * xf, axis=-1, keepdims=True) + EPS)
    return (y * g.astype(jnp.float32)).astype(x.dtype)


def layer_norm(x, g, b):
    xf = x.astype(jnp.float32)
    mu = jnp.mean(xf, axis=-1, keepdims=True)
    var = jnp.mean(jnp.square(xf - mu), axis=-1, keepdims=True)
    y = (xf - mu) * lax.rsqrt(var + EPS)
    return (y * g.astype(jnp.float32) + b.astype(jnp.float32)).astype(x.dtype)


def causal_depthwise_conv(x, w, b):
    width, chans = w.shape
    y = lax.conv_general_dilated(x, w[:, None, :].astype(x.dtype), window_strides=(1,),
                                 padding=[(width - 1, 0)],
                                 dimension_numbers=('NWC', 'WIO', 'NWC'),
                                 feature_group_count=chans)
    return y + b


def rope(x, cos, sin):
    x1, x2 = jnp.split(x, 2, axis=-1)
    return jnp.concatenate([x1 * cos - x2 * sin, x1 * sin + x2 * cos], axis=-1)


def conformer_branch(u_glu, gate, dw_w, dw_b, ln_g, ln_b, w_proj):
    a, b = jnp.split(u_glu, 2, axis=-1)
    y = a * jax.nn.sigmoid(b)
    y = causal_depthwise_conv(y, dw_w, dw_b)
    y = layer_norm(y, ln_g, ln_b)
    y = jax.nn.silu(y) * jax.nn.silu(gate)
    return y @ w_proj


def causal_block_attention(q, k, v):
    B, S, H, _ = q.shape
    nb = S // Q_BLOCK
    scale = QK_DIM ** -0.5
    qb = q.reshape(B, nb, Q_BLOCK, H, QK_DIM).swapaxes(0, 1)
    k_pos = jnp.arange(S)
    neg = jnp.finfo(jnp.float32).min

    def one_block(args):
        q_blk, i = args
        s = jnp.einsum('bqhd,bkhd->bhqk', q_blk, k, preferred_element_type=jnp.float32) * scale
        q_pos = i * Q_BLOCK + jnp.arange(Q_BLOCK)
        s = jnp.where(k_pos[None, :] <= q_pos[:, None], s, neg)
        p = jax.nn.softmax(s, axis=-1).astype(v.dtype)
        return jnp.einsum('bhqk,bkhd->bqhd', p, v)

    o = lax.map(one_block, (qb, jnp.arange(nb)))
    return o.swapaxes(0, 1).reshape(B, S, H * V_DIM)


def mla_branch(c_q, c_kv, k_r, gate, cos, sin, q_norm_g, w_uq, kv_norm_g, w_ukv, w_proj):
    B, S, _ = c_q.shape
    q = (rms_norm(c_q, q_norm_g) @ w_uq).reshape(B, S, N_HEADS, QK_DIM)
    q_nope, q_rope = jnp.split(q, [QK_NOPE_DIM], axis=-1)
    q = jnp.concatenate([q_nope, rope(q_rope, cos[:, :, None, :], sin[:, :, None, :])], axis=-1)
    kv = (rms_norm(c_kv, kv_norm_g) @ w_ukv).reshape(B, S, N_HEADS, QK_NOPE_DIM + V_DIM)
    k_nope, v = jnp.split(kv, [QK_NOPE_DIM], axis=-1)
    k_rope = rope(k_r, cos, sin)[:, :, None, :]
    k = jnp.concatenate([k_nope, jnp.broadcast_to(k_rope, (B, S, N_HEADS, QK_ROPE_DIM))], axis=-1)
    o = causal_block_attention(q, k, v)
    return (o * jax.nn.silu(gate)) @ w_proj


def rg_lru_branch(u, gate, cw, cb, w_a, b_a, w_x, b_x, lam, w_proj):
    B, S, _ = u.shape
    xc = causal_depthwise_conv(u, cw, cb)
    xb = xc.reshape(B, S, LRU_BLOCKS, LRU_BLOCK_DIM)
    r = jax.nn.sigmoid(jnp.einsum('bsgi,gij->bsgj', xb, w_a).reshape(B, S, D_LRU) + b_a)
    i = jax.nn.sigmoid(jnp.einsum('bsgi,gij->bsgj', xb, w_x).reshape(B, S, D_LRU) + b_x)
    log_a = -LRU_C * r.astype(jnp.float32) * jax.nn.softplus(-lam.astype(jnp.float32))
    a = jnp.exp(log_a)
    mult = jnp.sqrt(-jnp.expm1(2.0 * log_a))
    b = mult * (i * xc).astype(jnp.float32)

    def combine(c1, c2):
        a1, b1 = c1
        a2, b2 = c2
        return a1 * a2, a2 * b1 + b2

    _, h = lax.associative_scan(combine, (a, b), axis=1)
    y = h.astype(u.dtype) * jax.nn.silu(gate)
    return y @ w_proj


def setup_inputs(seed: int = 0) -> dict:
    key = jax.random.key(seed)
    ks = jax.random.split(key, 32)
    f32 = jnp.float32

    def w(k, shape, fan_in):
        return jax.random.normal(k, shape, f32) * (fan_in ** -0.5)

    def gain(k, shape):
        return 1.0 + 0.02 * jax.random.normal(k, shape, f32)

    def bias(k, shape):
        return 0.01 * jax.random.normal(k, shape, f32)

    u = jax.random.uniform(ks[20], (DEPTH, D_LRU), f32, minval=0.9, maxval=0.999)
    a0 = u ** (1.0 / LRU_C)
    lru_lambda = jnp.log(a0) - jnp.log1p(-a0)

    return {
        'x': jax.random.normal(ks[0], (BATCH, SEQ, D_MODEL), f32),
        'positions': jnp.broadcast_to(jnp.arange(SEQ, dtype=jnp.int32), (BATCH, SEQ)),
        'pre_norm_g': gain(ks[1], (DEPTH, D_MODEL)),
        'w_in': w(ks[2], (DEPTH, D_MODEL, D_IN), D_MODEL),
        'conv_dw_w': w(ks[3], (DEPTH, CONV_WIDTH, D_CONV), CONV_WIDTH),
        'conv_dw_b': bias(ks[4], (DEPTH, D_CONV)),
        'conv_ln_g': gain(ks[5], (DEPTH, D_CONV)),
        'conv_ln_b': bias(ks[6], (DEPTH, D_CONV)),
        'w_conv_proj': w(ks[7], (DEPTH, D_CONV, D_MODEL), D_CONV),
        'q_norm_g': gain(ks[8], (DEPTH, Q_LORA)),
        'w_uq': w(ks[9], (DEPTH, Q_LORA, N_HEADS * QK_DIM), Q_LORA),
        'kv_norm_g': gain(ks[10], (DEPTH, KV_LORA)),
        'w_ukv': w(ks[11], (DEPTH, KV_LORA, N_HEADS * (QK_NOPE_DIM + V_DIM)), KV_LORA),
        'w_mla_proj': w(ks[12], (DEPTH, D_MLA, D_MODEL), D_MLA),
        'lru_conv_w': w(ks[13], (DEPTH, LRU_CONV_WIDTH, D_LRU), LRU_CONV_WIDTH),
        'lru_conv_b': bias(ks[14], (DEPTH, D_LRU)),
        'lru_w_a': w(ks[15], (DEPTH, LRU_BLOCKS, LRU_BLOCK_DIM, LRU_BLOCK_DIM), LRU_BLOCK_DIM),
        'lru_b_a': bias(ks[16], (DEPTH, D_LRU)),
        'lru_w_x': w(ks[17], (DEPTH, LRU_BLOCKS, LRU_BLOCK_DIM, LRU_BLOCK_DIM), LRU_BLOCK_DIM),
        'lru_b_x': bias(ks[18], (DEPTH, D_LRU)),
        'lru_lambda': lru_lambda,
        'w_lru_proj': w(ks[19], (DEPTH, D_LRU, D_MODEL), D_LRU),
        'w_out': w(ks[21], (DEPTH, D_MODEL, D_MODEL), D_MODEL),
        'post_norm_g': gain(ks[22], (DEPTH, D_MODEL)),
    }


def reference(x, positions, pre_norm_g, w_in, conv_dw_w, conv_dw_b, conv_ln_g, conv_ln_b,
              w_conv_proj, q_norm_g, w_uq, kv_norm_g, w_ukv, w_mla_proj, lru_conv_w, lru_conv_b,
              lru_w_a, lru_b_a, lru_w_x, lru_b_x, lru_lambda, w_lru_proj, w_out, post_norm_g):
    B, S, _ = x.shape
    inv_freq = ROPE_THETA ** (-jnp.arange(0, QK_ROPE_DIM, 2, dtype=jnp.float32) / QK_ROPE_DIM)
    ang = positions.astype(jnp.float32)[..., None] * inv_freq
    cos = jnp.cos(ang).astype(x.dtype)
    sin = jnp.sin(ang).astype(x.dtype)
    split_at = np.cumsum(IN_SPLITS)[:-1].tolist()

    for l in range(DEPTH):
        h = rms_norm(x, pre_norm_g[l])
        proj = h @ w_in[l]
        (u_glu, g_conv, c_q, c_kv, k_r, g_mla, u_lru, g_lru, g_merge) = jnp.split(proj, split_at, axis=-1)

        y_a = conformer_branch(u_glu, g_conv, conv_dw_w[l], conv_dw_b[l], conv_ln_g[l],
                               conv_ln_b[l], w_conv_proj[l])
        y_b = mla_branch(c_q, c_kv, k_r, g_mla, cos, sin, q_norm_g[l], w_uq[l], kv_norm_g[l],
                         w_ukv[l], w_mla_proj[l])
        y_c = rg_lru_branch(u_lru, g_lru, lru_conv_w[l], lru_conv_b[l], lru_w_a[l], lru_b_a[l],
                            lru_w_x[l], lru_b_x[l], lru_lambda[l], w_lru_proj[l])

        gm = jax.nn.sigmoid(g_merge.reshape(B, S, N_BRANCH, D_MODEL))
        merged = gm[:, :, 0] * y_a + gm[:, :, 1] * y_b + gm[:, :, 2] * y_c
        out = merged @ w_out[l]
        x = x + rms_norm(out, post_norm_g[l])
    return x
```

```python
import functools

import jax
import jax.numpy as jnp
import numpy as np
from jax import lax
from jax.experimental import pallas as pl
from jax.experimental.pallas import tpu as pltpu

N_HEADS = 32
QK_NOPE_DIM = 128
QK_ROPE_DIM = 64
V_DIM = 128
CONV_WIDTH = 31
LRU_CONV_WIDTH = 4
LRU_BLOCKS = 16
LRU_C = 8.0
ROPE_THETA = 10000.0
EPS = 1e-6

LANES = 128
SUBLANES = 8
CONV_HALO = 32
VMEM_LIMIT_BYTES = 52 * 1024 * 1024

F32 = jnp.float32
BF16 = jnp.bfloat16


def _cparams(*sem):
    return pltpu.CompilerParams(dimension_semantics=sem, vmem_limit_bytes=VMEM_LIMIT_BYTES)


def _tile(dim, pref):
    t = min(dim, pref)
    assert dim % t == 0, (dim, t)
    return t


def _rms(x, g):
    return x * lax.rsqrt(jnp.mean(x * x, axis=-1, keepdims=True) + EPS) * g


def _silu(x):
    return x * jax.nn.sigmoid(x)


def _rope_lanes(x, cos, sin_signed):
    lane = lax.broadcasted_iota(jnp.int32, x.shape, 1)
    first_half = (lane % QK_ROPE_DIM) < (QK_ROPE_DIM // 2)
    partner = jnp.where(first_half,
                        pltpu.roll(x, LANES - QK_ROPE_DIM // 2, 1),
                        pltpu.roll(x, QK_ROPE_DIM // 2, 1))
    return x * cos + partner * sin_signed


def _rope_table_kernel(pos_ref, freq_ref, sign_ref, cos_ref, sin_ref):
    ang = pos_ref[...].astype(F32) * freq_ref[...]
    cos_ref[...] = jnp.cos(ang)
    sin_ref[...] = jnp.sin(ang) * sign_ref[...]


def _rope_tables(positions):
    s = positions.shape[0]
    half = QK_ROPE_DIM // 2
    inv_freq = ROPE_THETA ** (-jnp.arange(0, QK_ROPE_DIM, 2, dtype=F32) / QK_ROPE_DIM)
    freq = jnp.tile(inv_freq, LANES // half)[None, :]
    sign = jnp.tile(jnp.concatenate([-jnp.ones((half,), F32), jnp.ones((half,), F32)]),
                    LANES // QK_ROPE_DIM)[None, :]
    t = _tile(s, 512)
    row = pl.BlockSpec((1, LANES), lambda i: (0, 0))
    out = pl.BlockSpec((t, LANES), lambda i: (i, 0))
    return pl.pallas_call(
        _rope_table_kernel,
        out_shape=(jax.ShapeDtypeStruct((s, LANES), F32),) * 2,
        grid=(s // t,),
        in_specs=[pl.BlockSpec((t, 1), lambda i: (i, 0)), row, row],
        out_specs=(out, out),
        compiler_params=_cparams("parallel"),
        name="rope_tables",
    )(positions.reshape(s, 1), freq, sign)


def _prenorm_kernel(x_ref, g_ref, h_ref):
    h_ref[...] = _rms(x_ref[...], g_ref[...]).astype(h_ref.dtype)


def _prenorm(x, g):
    s, d = x.shape
    t = _tile(s, 256)
    return pl.pallas_call(
        _prenorm_kernel,
        out_shape=jax.ShapeDtypeStruct((s, d), BF16),
        grid=(s // t,),
        in_specs=[pl.BlockSpec((t, d), lambda i: (i, 0)), pl.BlockSpec((1, d), lambda i: (0, 0))],
        out_specs=pl.BlockSpec((t, d), lambda i: (i, 0)),
        compiler_params=_cparams("parallel"),
        name="prenorm",
    )(x, g.reshape(1, d))


def _post_kernel(x_ref, o_ref, gpost_ref, gpre_ref, xn_ref, h_ref):
    xn = x_ref[...] + _rms(o_ref[...], gpost_ref[...])
    xn_ref[...] = xn
    h_ref[...] = _rms(xn, gpre_ref[...]).astype(h_ref.dtype)


def _post(x, out, g_post, g_pre_next):
    s, d = x.shape
    t = _tile(s, 256)
    tile = pl.BlockSpec((t, d), lambda i: (i, 0))
    row = pl.BlockSpec((1, d), lambda i: (0, 0))
    return pl.pallas_call(
        _post_kernel,
        out_shape=(jax.ShapeDtypeStruct((s, d), F32), jax.ShapeDtypeStruct((s, d), BF16)),
        grid=(s // t,),
        in_specs=[tile, tile, row, row],
        out_specs=(tile, tile),
        compiler_params=_cparams("parallel"),
        name="post",
    )(x, out, g_post.reshape(1, d), g_pre_next.reshape(1, d))


def _proj_kernel(a_ref, w_ref, o_ref, *, act):
    acc = jnp.dot(a_ref[...], w_ref[...], preferred_element_type=F32)
    if act == "silu":
        acc = _silu(acc)
    elif act == "sigmoid":
        acc = jax.nn.sigmoid(acc)
    o_ref[...] = acc.astype(o_ref.dtype)


def _proj(a, w, out_dtype, act=None, tm=1024, tn=512, name="proj"):
    m, k = a.shape
    n = w.shape[1]
    tm, tn = _tile(m, tm), _tile(n, tn)
    return pl.pallas_call(
        functools.partial(_proj_kernel, act=act),
        out_shape=jax.ShapeDtypeStruct((m, n), out_dtype),
        grid=(m // tm, n // tn),
        in_specs=[pl.BlockSpec((tm, k), lambda i, j: (i, 0)),
                  pl.BlockSpec((k, tn), lambda i, j: (0, j))],
        out_specs=pl.BlockSpec((tm, tn), lambda i, j: (i, j)),
        compiler_params=_cparams("parallel", "parallel"),
        name=name,
    )(a, w)


def _glu_kernel(a_ref, wa_ref, wb_ref, o_ref):
    a = a_ref[...]
    ya = jnp.dot(a, wa_ref[...], preferred_element_type=F32)
    yb = jnp.dot(a, wb_ref[...], preferred_element_type=F32)
    o_ref[...] = ya * jax.nn.sigmoid(yb)


def _glu_proj(a, w_glu, tm=1024, tn=256):
    m, k = a.shape
    n = w_glu.shape[1] // 2
    tm, tn = _tile(m, tm), _tile(n, tn)
    nb = n // tn
    return pl.pallas_call(
        _glu_kernel,
        out_shape=jax.ShapeDtypeStruct((m, n), F32),
        grid=(m // tm, nb),
        in_specs=[pl.BlockSpec((tm, k), lambda i, j: (i, 0)),
                  pl.BlockSpec((k, tn), lambda i, j: (0, j)),
                  pl.BlockSpec((k, tn), lambda i, j: (0, j + nb))],
        out_specs=pl.BlockSpec((tm, tn), lambda i, j: (i, j)),
        compiler_params=_cparams("parallel", "parallel"),
        name="glu_proj",
    )(a, w_glu, w_glu)


def _cq_kernel(a_ref, w_ref, g_ref, o_ref):
    acc = jnp.dot(a_ref[...], w_ref[...], preferred_element_type=F32)
    o_ref[...] = _rms(acc, g_ref[...]).astype(o_ref.dtype)


def _cq_proj(a, w, g, tm=512):
    m, k = a.shape
    n = w.shape[1]
    tm = _tile(m, tm)
    return pl.pallas_call(
        _cq_kernel,
        out_shape=jax.ShapeDtypeStruct((m, n), BF16),
        grid=(m // tm,),
        in_specs=[pl.BlockSpec((tm, k), lambda i: (i, 0)),
                  pl.BlockSpec((k, n), lambda i: (0, 0)),
                  pl.BlockSpec((1, n), lambda i: (0, 0))],
        out_specs=pl.BlockSpec((tm, n), lambda i: (i, 0)),
        compiler_params=_cparams("parallel"),
        name="cq_proj",
    )(a, w, g.reshape(1, n))


def _ckvr_kernel(a_ref, w_ref, g_ref, cos_ref, sin_ref, ckv_ref, kr_ref, *, kv_lora):
    acc = jnp.dot(a_ref[...], w_ref[...], preferred_element_type=F32)
    ckv_ref[...] = _rms(acc[:, :kv_lora], g_ref[...]).astype(ckv_ref.dtype)
    kr_ref[...] = _rope_lanes(acc[:, kv_lora:], cos_ref[...], sin_ref[...]).astype(kr_ref.dtype)


def _ckvr_proj(a, w, g, cos, sin, tm=512):
    m, k = a.shape
    n = w.shape[1]
    kv_lora = n - LANES
    tm = _tile(m, tm)
    return pl.pallas_call(
        functools.partial(_ckvr_kernel, kv_lora=kv_lora),
        out_shape=(jax.ShapeDtypeStruct((m, kv_lora), BF16), jax.ShapeDtypeStruct((m, LANES), BF16)),
        grid=(m // tm,),
        in_specs=[pl.BlockSpec((tm, k), lambda i: (i, 0)),
                  pl.BlockSpec((k, n), lambda i: (0, 0)),
                  pl.BlockSpec((1, kv_lora), lambda i: (0, 0)),
                  pl.BlockSpec((tm, LANES), lambda i: (i, 0)),
                  pl.BlockSpec((tm, LANES), lambda i: (i, 0))],
        out_specs=(pl.BlockSpec((tm, kv_lora), lambda i: (i, 0)),
                   pl.BlockSpec((tm, LANES), lambda i: (i, 0))),
        compiler_params=_cparams("parallel"),
        name="ckvr_proj",
    )(a, w, g.reshape(1, kv_lora), cos, sin)


def _conv_branch_kernel(cur_ref, prev_ref, gate_ref, w_ref, b_ref, lng_ref, lnb_ref, o_ref,
                        buf_ref, conv_ref, *, t):
    i = pl.program_id(0)
    c = cur_ref.shape[1]
    prev = prev_ref[...]
    buf_ref[0:CONV_HALO, :] = jnp.where(i > 0, prev, jnp.zeros_like(prev))
    buf_ref[CONV_HALO:, :] = cur_ref[...]
    aligned = CONV_HALO - SUBLANES
    for c0 in range(0, c, LANES):
        lanes = slice(c0, c0 + LANES)
        acc = jnp.zeros((t, LANES), F32)
        for r in range(SUBLANES):
            z = buf_ref[pl.ds(SUBLANES - r, t + aligned), lanes]
            for q in range(CONV_HALO // SUBLANES):
                d = SUBLANES * q + r
                if d >= CONV_WIDTH:
                    continue
                off = aligned - SUBLANES * q
                acc = acc + w_ref[pl.ds(CONV_WIDTH - 1 - d, 1), lanes] * z[off:off + t, :]
        conv_ref[:, lanes] = acc + b_ref[:, lanes]
    y = conv_ref[...]
    mu = jnp.mean(y, axis=-1, keepdims=True)
    yc = y - mu
    var = jnp.mean(yc * yc, axis=-1, keepdims=True)
    y = yc * lax.rsqrt(var + EPS) * lng_ref[...] + lnb_ref[...]
    o_ref[...] = (_silu(y) * gate_ref[...].astype(F32)).astype(o_ref.dtype)


def _conv_branch(y_glu, gates, gate_col, dw_w, dw_b, ln_g, ln_b, t=128):
    s, c = y_glu.shape
    t = _tile(s, t)
    assert t % CONV_HALO == 0 and gate_col % c == 0
    w = jnp.pad(dw_w, ((0, CONV_HALO - CONV_WIDTH), (0, 0)))
    row = pl.BlockSpec((1, c), lambda i: (0, 0))
    gblk = gate_col // c
    return pl.pallas_call(
        functools.partial(_conv_branch_kernel, t=t),
        out_shape=jax.ShapeDtypeStruct((s, c), BF16),
        grid=(s // t,),
        in_specs=[pl.BlockSpec((t, c), lambda i: (i, 0)),
                  pl.BlockSpec((CONV_HALO, c), lambda i: (jnp.maximum(i * (t // CONV_HALO) - 1, 0), 0)),
                  pl.BlockSpec((t, c), lambda i: (i, gblk)),
                  pl.BlockSpec((CONV_HALO, c), lambda i: (0, 0)),
                  row, row, row],
        out_specs=pl.BlockSpec((t, c), lambda i: (i, 0)),
        scratch_shapes=[pltpu.VMEM((t + CONV_HALO, c), F32), pltpu.VMEM((t, c), F32)],
        compiler_params=_cparams("parallel"),
        name="conv_branch",
    )(y_glu, y_glu, gates, w, dw_b.reshape(1, c), ln_g.reshape(1, c), ln_b.reshape(1, c))


def _lru_kernel(cur_ref, prev_ref, gate_ref, cw_ref, cb_ref, wa_ref, ba_ref, wx_ref, bx_ref,
                sp_ref, o_ref, buf_ref, a_scr, b_scr, h_scr, carry_ref, *, t):
    i = pl.program_id(0)
    c = cur_ref.shape[1]
    prev = prev_ref[...]
    buf_ref[0:SUBLANES, :] = jnp.where(i > 0, prev, jnp.zeros_like(prev))
    buf_ref[SUBLANES:, :] = cur_ref[...]

    @pl.when(i == 0)
    def _():
        carry_ref[...] = jnp.zeros_like(carry_ref)

    xc = jnp.zeros((t, c), F32) + cb_ref[...]
    for k in range(LRU_CONV_WIDTH):
        off = SUBLANES - (LRU_CONV_WIDTH - 1) + k
        xc = xc + cw_ref[pl.ds(k, 1), :] * buf_ref[pl.ds(off, t), :]

    xb = xc.astype(BF16)
    nblk = wa_ref.shape[0]
    bd = c // nblk
    ra = jnp.concatenate(
        [jnp.dot(xb[:, g * bd:(g + 1) * bd], wa_ref[g], preferred_element_type=F32) for g in range(nblk)],
        axis=1)
    rx = jnp.concatenate(
        [jnp.dot(xb[:, g * bd:(g + 1) * bd], wx_ref[g], preferred_element_type=F32) for g in range(nblk)],
        axis=1)
    r = jax.nn.sigmoid(ra + ba_ref[...])
    gi = jax.nn.sigmoid(rx + bx_ref[...])
    a = jnp.exp((-LRU_C) * r * sp_ref[...])
    b = jnp.sqrt(1.0 - a * a) * (gi * xc)

    a3 = a.reshape(t // SUBLANES, SUBLANES, c)
    b3 = b.reshape(t // SUBLANES, SUBLANES, c)
    sub = lax.broadcasted_iota(jnp.int32, a3.shape, 1)
    for d in (1, 2, 4):
        keep = sub >= d
        a_sh = pltpu.roll(a3, d, 1)
        b_sh = pltpu.roll(b3, d, 1)
        b3 = jnp.where(keep, a3 * b_sh + b3, b3)
        a3 = jnp.where(keep, a3 * a_sh, a3)
    a_scr[...] = a3
    b_scr[...] = b3

    h_last = carry_ref[...]
    for j in range(t // SUBLANES):
        hj = a_scr[j] * h_last + b_scr[j]
        h_scr[j] = hj
        h_last = hj[SUBLANES - 1:SUBLANES, :]
    carry_ref[...] = h_last

    h = h_scr[...].reshape(t, c)
    o_ref[...] = (h * gate_ref[...].astype(F32)).astype(o_ref.dtype)


def _lru_branch(u, gates, gate_col, cw, cb, w_a, b_a, w_x, b_x, lam, t=256):
    s, c = u.shape
    t = _tile(s, t)
    assert gate_col % c == 0
    nblk, bd, _ = w_a.shape
    row = pl.BlockSpec((1, c), lambda i: (0, 0))
    wblk = pl.BlockSpec((nblk, bd, bd), lambda i: (0, 0, 0))
    gblk = gate_col // c
    cw_pad = jnp.pad(cw, ((0, SUBLANES - LRU_CONV_WIDTH), (0, 0)))
    return pl.pallas_call(
        functools.partial(_lru_kernel, t=t),
        out_shape=jax.ShapeDtypeStruct((s, c), BF16),
        grid=(s // t,),
        in_specs=[pl.BlockSpec((t, c), lambda i: (i, 0)),
                  pl.BlockSpec((SUBLANES, c), lambda i: (jnp.maximum(i * (t // SUBLANES) - 1, 0), 0)),
                  pl.BlockSpec((t, c), lambda i: (i, gblk)),
                  pl.BlockSpec((SUBLANES, c), lambda i: (0, 0)),
                  row, wblk, row, wblk, row, row],
        out_specs=pl.BlockSpec((t, c), lambda i: (i, 0)),
        scratch_shapes=[pltpu.VMEM((t + SUBLANES, c), F32),
                        pltpu.VMEM((t // SUBLANES, SUBLANES, c), F32),
                        pltpu.VMEM((t // SUBLANES, SUBLANES, c), F32),
                        pltpu.VMEM((t // SUBLANES, SUBLANES, c), F32),
                        pltpu.VMEM((1, c), F32)],
        compiler_params=_cparams("arbitrary"),
        name="lru_branch",
    )(u, u, gates, cw_pad, cb.reshape(1, c), w_a, b_a.reshape(1, c), w_x, b_x.reshape(1, c),
      jax.nn.softplus(-lam).reshape(1, c))


HEAD_GROUP = 4
QK_PAD = 2 * LANES


def _q_kernel(a_ref, wn_ref, wr_ref, cos_ref, sin_ref, o_ref, *, scale):
    a = a_ref[...]
    qn = jnp.dot(a, wn_ref[...], preferred_element_type=F32) * scale
    qr = jnp.dot(a, wr_ref[...], preferred_element_type=F32)
    cos, sin = cos_ref[...], sin_ref[...]
    lane = lax.broadcasted_iota(jnp.int32, cos.shape, 1)
    for pair in range(HEAD_GROUP // 2):
        roped = _rope_lanes(qr[:, pair * LANES:(pair + 1) * LANES], cos, sin) * scale
        halves = (jnp.where(lane < QK_ROPE_DIM, roped, 0.0), jnp.where(lane >= QK_ROPE_DIM, roped, 0.0))
        for sub in range(2):
            h = 2 * pair + sub
            o_ref[h, :, 0:QK_NOPE_DIM] = qn[:, h * QK_NOPE_DIM:(h + 1) * QK_NOPE_DIM].astype(o_ref.dtype)
            o_ref[h, :, QK_NOPE_DIM:QK_PAD] = halves[sub].astype(o_ref.dtype)


def _q_proj(cqn, w_qn, w_qr, cos, sin, scale, tm=1024):
    m, k = cqn.shape
    nh = w_qn.shape[1] // QK_NOPE_DIM
    assert nh % HEAD_GROUP == 0
    tm = _tile(m, tm)
    return pl.pallas_call(
        functools.partial(_q_kernel, scale=scale),
        out_shape=jax.ShapeDtypeStruct((nh, m, QK_PAD), BF16),
        grid=(m // tm, nh // HEAD_GROUP),
        in_specs=[pl.BlockSpec((tm, k), lambda i, g: (i, 0)),
                  pl.BlockSpec((k, HEAD_GROUP * QK_NOPE_DIM), lambda i, g: (0, g)),
                  pl.BlockSpec((k, HEAD_GROUP * QK_ROPE_DIM), lambda i, g: (0, g)),
                  pl.BlockSpec((tm, LANES), lambda i, g: (i, 0)),
                  pl.BlockSpec((tm, LANES), lambda i, g: (i, 0))],
        out_specs=pl.BlockSpec((HEAD_GROUP, tm, QK_PAD), lambda i, g: (g, i, 0)),
        compiler_params=_cparams("parallel", "parallel"),
        name="q_proj",
    )(cqn, w_qn, w_qr, cos, sin)


def _kv_kernel(a_ref, wk_ref, wv_ref, kr_ref, k_ref, v_ref):
    a = a_ref[...]
    kn = jnp.dot(a, wk_ref[...], preferred_element_type=F32)
    v = jnp.dot(a, wv_ref[...], preferred_element_type=F32)
    kr = kr_ref[...]
    for h in range(HEAD_GROUP):
        k_ref[h, :, 0:QK_NOPE_DIM] = kn[:, h * QK_NOPE_DIM:(h + 1) * QK_NOPE_DIM].astype(k_ref.dtype)
        k_ref[h, :, QK_NOPE_DIM:QK_PAD] = kr
        v_ref[h] = v[:, h * V_DIM:(h + 1) * V_DIM].astype(v_ref.dtype)


def _kv_proj(ckvn, w_kn, w_v, kr, tm=1024):
    m, k = ckvn.shape
    nh = w_kn.shape[1] // QK_NOPE_DIM
    tm = _tile(m, tm)
    return pl.pallas_call(
        _kv_kernel,
        out_shape=(jax.ShapeDtypeStruct((nh, m, QK_PAD), BF16), jax.ShapeDtypeStruct((nh, m, V_DIM), BF16)),
        grid=(m // tm, nh // HEAD_GROUP),
        in_specs=[pl.BlockSpec((tm, k), lambda i, g: (i, 0)),
                  pl.BlockSpec((k, HEAD_GROUP * QK_NOPE_DIM), lambda i, g: (0, g)),
                  pl.BlockSpec((k, HEAD_GROUP * V_DIM), lambda i, g: (0, g)),
                  pl.BlockSpec((tm, LANES), lambda i, g: (i, 0))],
        out_specs=(pl.BlockSpec((HEAD_GROUP, tm, QK_PAD), lambda i, g: (g, i, 0)),
                   pl.BlockSpec((HEAD_GROUP, tm, V_DIM), lambda i, g: (g, i, 0))),
        compiler_params=_cparams("parallel", "parallel"),
        name="kv_proj",
    )(ckvn, w_kn, w_v, kr)


def _flash_kernel(q_ref, k_ref, v_ref, g_ref, o_ref, m_scr, l_scr, acc_scr, *, tq):
    qi = pl.program_id(1)
    q = q_ref[...]
    m_scr[...] = jnp.full_like(m_scr, -jnp.inf)
    l_scr[...] = jnp.zeros_like(l_scr)
    acc_scr[...] = jnp.zeros_like(acc_scr)

    def step(ki, diagonal):
        off = pl.multiple_of(ki * tq, tq)
        k = k_ref[pl.ds(off, tq), :]
        v = v_ref[pl.ds(off, tq), :]
        s = lax.dot_general(q, k, (((1,), (1,)), ((), ())), preferred_element_type=F32)
        if diagonal:
            row = lax.broadcasted_iota(jnp.int32, s.shape, 0)
            col = lax.broadcasted_iota(jnp.int32, s.shape, 1)
            s = jnp.where(col <= row, s, jnp.finfo(F32).min)
        m_prev = m_scr[...]
        m_new = jnp.maximum(m_prev, jnp.max(s, axis=-1, keepdims=True))
        alpha = jnp.exp(m_prev - m_new)
        p = jnp.exp(s - m_new)
        l_scr[...] = alpha * l_scr[...] + jnp.sum(p, axis=-1, keepdims=True)
        acc_scr[...] = alpha * acc_scr[...] + jnp.dot(p.astype(v.dtype), v, preferred_element_type=F32)
        m_scr[...] = m_new

    def body(ki, carry):
        step(ki, False)
        return carry

    lax.fori_loop(0, qi, body, 0)
    step(qi, True)
    o = acc_scr[...] / l_scr[...]
    o_ref[...] = (o * g_ref[...].astype(F32)).astype(o_ref.dtype)


def _flash(q, k, v, gates, gate_col, tq=512):
    nh, s, _ = q.shape
    tq = _tile(s, tq)
    assert gate_col % V_DIM == 0
    gblk = gate_col // V_DIM
    return pl.pallas_call(
        functools.partial(_flash_kernel, tq=tq),
        out_shape=jax.ShapeDtypeStruct((s, nh * V_DIM), BF16),
        grid=(nh, s // tq),
        in_specs=[pl.BlockSpec((None, tq, QK_PAD), lambda h, i: (h, i, 0)),
                  pl.BlockSpec((None, s, QK_PAD), lambda h, i: (h, 0, 0)),
                  pl.BlockSpec((None, s, V_DIM), lambda h, i: (h, 0, 0)),
                  pl.BlockSpec((tq, V_DIM), lambda h, i: (i, gblk + h))],
        out_specs=pl.BlockSpec((tq, V_DIM), lambda h, i: (i, h)),
        scratch_shapes=[pltpu.VMEM((tq, 1), F32), pltpu.VMEM((tq, 1), F32), pltpu.VMEM((tq, V_DIM), F32)],
        compiler_params=_cparams("parallel", "parallel"),
        name="flash",
    )(q, k, v, gates)


def _merge_kernel(ya_ref, yb_ref, yc_ref, wa_ref, wb_ref, wc_ref, g0_ref, g1_ref, g2_ref, o_ref):
    pa = jnp.dot(ya_ref[...], wa_ref[...], preferred_element_type=F32)
    pb = jnp.dot(yb_ref[...], wb_ref[...], preferred_element_type=F32)
    pc = jnp.dot(yc_ref[...], wc_ref[...], preferred_element_type=F32)
    merged = (g0_ref[...].astype(F32) * pa + g1_ref[...].astype(F32) * pb) + g2_ref[...].astype(F32) * pc
    o_ref[...] = merged.astype(o_ref.dtype)


def _merge(ya, yb, yc, wa, wb, wc, gm, tm=512, tn=512):
    m = ya.shape[0]
    n = wa.shape[1]
    tm, tn = _tile(m, tm), _tile(n, tn)
    nb = n // tn
    lhs = lambda y: pl.BlockSpec((tm, y.shape[1]), lambda i, j: (i, 0))
    rhs = lambda w: pl.BlockSpec((w.shape[0], tn), lambda i, j: (0, j))
    gate = lambda b: pl.BlockSpec((tm, tn), lambda i, j: (i, j + b * nb))
    return pl.pallas_call(
        _merge_kernel,
        out_shape=jax.ShapeDtypeStruct((m, n), BF16),
        grid=(m // tm, nb),
        in_specs=[lhs(ya), lhs(yb), lhs(yc), rhs(wa), rhs(wb), rhs(wc), gate(0), gate(1), gate(2)],
        out_specs=pl.BlockSpec((tm, tn), lambda i, j: (i, j)),
        compiler_params=_cparams("parallel", "parallel"),
        name="merge",
    )(ya, yb, yc, wa, wb, wc, gm, gm, gm)


def kernel(x, positions, pre_norm_g, w_in, conv_dw_w, conv_dw_b, conv_ln_g, conv_ln_b, w_conv_proj, q_norm_g, w_uq, kv_norm_g, w_ukv, w_mla_proj, lru_conv_w, lru_conv_b, lru_w_a, lru_b_a, lru_w_x, lru_b_x, lru_lambda, w_lru_proj, w_out, post_norm_g):
    batch, seq, d_model = x.shape
    assert batch == 1
    depth = w_in.shape[0]
    nh = N_HEADS
    d_conv = conv_dw_w.shape[2]
    d_lru = lru_conv_w.shape[2]
    q_lora = w_uq.shape[1]
    kv_lora = w_ukv.shape[1]
    d_mla = nh * V_DIM
    qk_dim = QK_NOPE_DIM + QK_ROPE_DIM
    scale = qk_dim ** -0.5
    splits = np.cumsum([0, 2 * d_conv, d_conv, q_lora, kv_lora, QK_ROPE_DIM, d_mla, d_lru, d_lru, 3 * d_model])
    (c_glu, c_gconv, c_cq, c_ckv, c_kr, c_gmla, c_ulru, c_glru, c_gm, c_end) = [int(v) for v in splits]
    assert c_end == w_in.shape[2]

    cos, sin = _rope_tables(positions[0])
    xs = x[0]
    h = _prenorm(xs, pre_norm_g[0])

    for l in range(depth):
        wl = w_in[l]
        cast = lambda w: w.astype(BF16)
        w_glu = cast(wl[:, c_glu:c_gconv])
        w_cq = cast(wl[:, c_cq:c_ckv])
        w_kr = wl[:, c_kr:c_gmla]
        w_ckvr = cast(jnp.concatenate([wl[:, c_ckv:c_kr], w_kr, w_kr], axis=1))
        w_ulru = cast(wl[:, c_ulru:c_glru])
        w_gates = cast(jnp.concatenate([wl[:, c_gconv:c_cq], wl[:, c_gmla:c_ulru], wl[:, c_glru:c_gm]], axis=1))
        w_gm = cast(wl[:, c_gm:c_end])
        uq = w_uq[l].reshape(q_lora, nh, qk_dim)
        w_qn = cast(uq[:, :, :QK_NOPE_DIM].reshape(q_lora, nh * QK_NOPE_DIM))
        w_qr = cast(uq[:, :, QK_NOPE_DIM:].reshape(q_lora, nh * QK_ROPE_DIM))
        ukv = w_ukv[l].reshape(kv_lora, nh, QK_NOPE_DIM + V_DIM)
        w_kn = cast(ukv[:, :, :QK_NOPE_DIM].reshape(kv_lora, nh * QK_NOPE_DIM))
        w_v = cast(ukv[:, :, QK_NOPE_DIM:].reshape(kv_lora, nh * V_DIM))

        y_glu = _glu_proj(h, w_glu)
        cqn = _cq_proj(h, w_cq, q_norm_g[l])
        ckvn, kr = _ckvr_proj(h, w_ckvr, kv_norm_g[l], cos, sin)
        u_lru = _proj(h, w_ulru, F32, name="ulru_proj")
        gates = _proj(h, w_gates, BF16, act="silu", name="gate_proj")
        gm = _proj(h, w_gm, BF16, act="sigmoid", name="merge_gate_proj")

        y_a = _conv_branch(y_glu, gates, 0, conv_dw_w[l], conv_dw_b[l], conv_ln_g[l], conv_ln_b[l])
        q = _q_proj(cqn, w_qn, w_qr, cos, sin, scale)
        k, v = _kv_proj(ckvn, w_kn, w_v, kr)
        y_b = _flash(q, k, v, gates, d_conv)
        y_c = _lru_branch(u_lru, gates, d_conv + d_mla, lru_conv_w[l], lru_conv_b[l], cast(lru_w_a[l]),
                          lru_b_a[l], cast(lru_w_x[l]), lru_b_x[l], lru_lambda[l])

        merged = _merge(y_a, y_b, y_c, cast(w_conv_proj[l]), cast(w_mla_proj[l]), cast(w_lru_proj[l]), gm)
        out = _proj(merged, cast(w_out[l]), F32, name="out_proj")
        g_next = pre_norm_g[l + 1] if l + 1 < depth else pre_norm_g[l]
        xs, h = _post(xs, out, post_norm_g[l], g_next)
    return xs[None]
```

```python
import functools

import jax
import jax.numpy as jnp
import numpy as np
from jax import lax
from jax.experimental import pallas as pl
from jax.experimental.pallas import tpu as pltpu

N_HEADS = 32
QK_NOPE_DIM = 128
QK_ROPE_DIM = 64
V_DIM = 128
CONV_WIDTH = 31
LRU_CONV_WIDTH = 4
LRU_BLOCKS = 16
LRU_C = 8.0
ROPE_THETA = 10000.0
EPS = 1e-6

LANES = 128
SUBLANES = 8
CONV_HALO = 32
VMEM_LIMIT_BYTES = 52 * 1024 * 1024

F32 = jnp.float32
BF16 = jnp.bfloat16


def _cparams(*sem):
    return pltpu.CompilerParams(dimension_semantics=sem, vmem_limit_bytes=VMEM_LIMIT_BYTES)


def _tile(dim, pref):
    t = min(dim, pref)
    assert dim % t == 0, (dim, t)
    return t


def _rms(x, g):
    return x * lax.rsqrt(jnp.mean(x * x, axis=-1, keepdims=True) + EPS) * g


def _silu(x):
    return x * jax.nn.sigmoid(x)


def _rope_lanes(x, cos, sin_signed):
    lane = lax.broadcasted_iota(jnp.int32, x.shape, 1)
    first_half = (lane % QK_ROPE_DIM) < (QK_ROPE_DIM // 2)
    partner = jnp.where(first_half,
                        pltpu.roll(x, LANES - QK_ROPE_DIM // 2, 1),
                        pltpu.roll(x, QK_ROPE_DIM // 2, 1))
    return x * cos + partner * sin_signed


def _rope_table_kernel(pos_ref, freq_ref, sign_ref, cos_ref, sin_ref):
    ang = pos_ref[...].astype(F32) * freq_ref[...]
    cos_ref[...] = jnp.cos(ang)
    sin_ref[...] = jnp.sin(ang) * sign_ref[...]


def _rope_tables(positions):
    s = positions.shape[0]
    half = QK_ROPE_DIM // 2
    inv_freq = ROPE_THETA ** (-jnp.arange(0, QK_ROPE_DIM, 2, dtype=F32) / QK_ROPE_DIM)
    freq = jnp.tile(inv_freq, LANES // half)[None, :]
    sign = jnp.tile(jnp.concatenate([-jnp.ones((half,), F32), jnp.ones((half,), F32)]),
                    LANES // QK_ROPE_DIM)[None, :]
    t = _tile(s, 512)
    row = pl.BlockSpec((1, LANES), lambda i: (0, 0))
    out = pl.BlockSpec((t, LANES), lambda i: (i, 0))
    return pl.pallas_call(
        _rope_table_kernel,
        out_shape=(jax.ShapeDtypeStruct((s, LANES), F32),) * 2,
        grid=(s // t,),
        in_specs=[pl.BlockSpec((t, 1), lambda i: (i, 0)), row, row],
        out_specs=(out, out),
        compiler_params=_cparams("parallel"),
        name="rope_tables",
    )(positions.reshape(s, 1), freq, sign)


def _prenorm_kernel(x_ref, g_ref, h_ref):
    h_ref[...] = _rms(x_ref[...], g_ref[...]).astype(h_ref.dtype)


def _prenorm(x, g):
    s, d = x.shape
    t = _tile(s, 256)
    return pl.pallas_call(
        _prenorm_kernel,
        out_shape=jax.ShapeDtypeStruct((s, d), BF16),
        grid=(s // t,),
        in_specs=[pl.BlockSpec((t, d), lambda i: (i, 0)), pl.BlockSpec((1, d), lambda i: (0, 0))],
        out_specs=pl.BlockSpec((t, d), lambda i: (i, 0)),
        compiler_params=_cparams("parallel"),
        name="prenorm",
    )(x, g.reshape(1, d))


def _post_kernel(x_ref, o_ref, gpost_ref, gpre_ref, xn_ref, h_ref):
    xn = x_ref[...] + _rms(o_ref[...], gpost_ref[...])
    xn_ref[...] = xn
    h_ref[...] = _rms(xn, gpre_ref[...]).astype(h_ref.dtype)


def _post(x, out, g_post, g_pre_next):
    s, d = x.shape
    t = _tile(s, 256)
    tile = pl.BlockSpec((t, d), lambda i: (i, 0))
    row = pl.BlockSpec((1, d), lambda i: (0, 0))
    return pl.pallas_call(
        _post_kernel,
        out_shape=(jax.ShapeDtypeStruct((s, d), F32), jax.ShapeDtypeStruct((s, d), BF16)),
        grid=(s // t,),
        in_specs=[tile, tile, row, row],
        out_specs=(tile, tile),
        compiler_params=_cparams("parallel"),
        name="post",
    )(x, out, g_post.reshape(1, d), g_pre_next.reshape(1, d))


HALF_LANES = LANES // 2


def _proj_kernel(a_ref, *refs, act, n_w, shifted):
    n_in = n_w * (2 if shifted else 1)
    w_refs, o_ref, w_scr = refs[:n_in], refs[n_in], refs[n_in + 1:]

    @pl.when(pl.program_id(1) == 0)
    def _():
        for i in range(n_w):
            if shifted:
                w = jnp.concatenate([w_refs[2 * i][:, HALF_LANES:], w_refs[2 * i + 1][:, :HALF_LANES]], axis=1)
            else:
                w = w_refs[i][...]
            w_scr[i][...] = w.astype(BF16)

    a = a_ref[...]
    acc = jnp.dot(a, w_scr[0][...], preferred_element_type=F32)
    if act == "glu":
        acc = acc * jax.nn.sigmoid(jnp.dot(a, w_scr[1][...], preferred_element_type=F32))
    elif act == "silu":
        acc = _silu(acc)
    elif act == "sigmoid":
        acc = jax.nn.sigmoid(acc)
    o_ref[...] = acc.astype(o_ref.dtype)


def _proj(a, w_all, layer, col0, ncols, out_dtype, act=None, tm=1024, tn=512, name="proj"):
    m, k = a.shape
    shifted = col0 % LANES == HALF_LANES
    base = col0 - (HALF_LANES if shifted else 0)
    assert base % LANES == 0
    tm = _tile(m, tm)
    tn = next(t for t in (tn, 256, LANES) if ncols % t == 0 and base % t == 0)
    n_w = 2 if act == "glu" else 1
    w_specs = []
    for i in range(n_w):
        blk0 = (base + i * ncols) // tn
        w_specs.append(pl.BlockSpec((None, k, tn), lambda j, r, blk0=blk0: (layer, 0, blk0 + j)))
        if shifted:
            per = tn // LANES
            w_specs.append(pl.BlockSpec((None, k, LANES), lambda j, r, blk0=blk0: (layer, 0, (blk0 + j + 1) * per)))
    return pl.pallas_call(
        functools.partial(_proj_kernel, act=act, n_w=n_w, shifted=shifted),
        out_shape=jax.ShapeDtypeStruct((m, ncols), out_dtype),
        grid=(ncols // tn, m // tm),
        in_specs=[pl.BlockSpec((tm, k), lambda j, r: (r, 0))] + w_specs,
        out_specs=pl.BlockSpec((tm, tn), lambda j, r: (r, j)),
        scratch_shapes=[pltpu.VMEM((k, tn), BF16)] * n_w,
        compiler_params=_cparams("parallel", "arbitrary"),
        name=name,
    )(a, *([w_all] * len(w_specs)))


def _cq_kernel(a_ref, w_ref, g_ref, o_ref):
    acc = jnp.dot(a_ref[...], w_ref[...], preferred_element_type=F32)
    o_ref[...] = _rms(acc, g_ref[...]).astype(o_ref.dtype)


def _cq_proj(a, w, g, tm=512):
    m, k = a.shape
    n = w.shape[1]
    tm = _tile(m, tm)
    return pl.pallas_call(
        _cq_kernel,
        out_shape=jax.ShapeDtypeStruct((m, n), BF16),
        grid=(m // tm,),
        in_specs=[pl.BlockSpec((tm, k), lambda i: (i, 0)),
                  pl.BlockSpec((k, n), lambda i: (0, 0)),
                  pl.BlockSpec((1, n), lambda i: (0, 0))],
        out_specs=pl.BlockSpec((tm, n), lambda i: (i, 0)),
        compiler_params=_cparams("parallel"),
        name="cq_proj",
    )(a, w, g.reshape(1, n))


def _ckvr_kernel(a_ref, w_ref, g_ref, cos_ref, sin_ref, ckv_ref, kr_ref, *, kv_lora):
    acc = jnp.dot(a_ref[...], w_ref[...], preferred_element_type=F32)
    ckv_ref[...] = _rms(acc[:, :kv_lora], g_ref[...]).astype(ckv_ref.dtype)
    kr_ref[...] = _rope_lanes(acc[:, kv_lora:], cos_ref[...], sin_ref[...]).astype(kr_ref.dtype)


def _ckvr_proj(a, w, g, cos, sin, tm=512):
    m, k = a.shape
    n = w.shape[1]
    kv_lora = n - LANES
    tm = _tile(m, tm)
    return pl.pallas_call(
        functools.partial(_ckvr_kernel, kv_lora=kv_lora),
        out_shape=(jax.ShapeDtypeStruct((m, kv_lora), BF16), jax.ShapeDtypeStruct((m, LANES), BF16)),
        grid=(m // tm,),
        in_specs=[pl.BlockSpec((tm, k), lambda i: (i, 0)),
                  pl.BlockSpec((k, n), lambda i: (0, 0)),
                  pl.BlockSpec((1, kv_lora), lambda i: (0, 0)),
                  pl.BlockSpec((tm, LANES), lambda i: (i, 0)),
                  pl.BlockSpec((tm, LANES), lambda i: (i, 0))],
        out_specs=(pl.BlockSpec((tm, kv_lora), lambda i: (i, 0)),
                   pl.BlockSpec((tm, LANES), lambda i: (i, 0))),
        compiler_params=_cparams("parallel"),
        name="ckvr_proj",
    )(a, w, g.reshape(1, kv_lora), cos, sin)


def _conv_branch_kernel(cur_ref, prev_ref, gate_ref, w_ref, b_ref, lng_ref, lnb_ref, o_ref,
                        buf_ref, conv_ref, *, t):
    i = pl.program_id(0)
    c = cur_ref.shape[1]
    prev = prev_ref[...]
    buf_ref[0:CONV_HALO, :] = jnp.where(i > 0, prev, jnp.zeros_like(prev))
    buf_ref[CONV_HALO:, :] = cur_ref[...]
    aligned = CONV_HALO - SUBLANES
    for c0 in range(0, c, LANES):
        lanes = slice(c0, c0 + LANES)
        acc = jnp.zeros((t, LANES), F32) + b_ref[:, lanes]
        for r in range(SUBLANES):
            z = buf_ref[pl.ds(SUBLANES - r, t + aligned), lanes]
            part = None
            for q in range(CONV_HALO // SUBLANES):
                d = SUBLANES * q + r
                if d >= CONV_WIDTH:
                    continue
                off = aligned - SUBLANES * q
                term = w_ref[pl.ds(CONV_WIDTH - 1 - d, 1), lanes] * z[off:off + t, :]
                part = term if part is None else part + term
            acc = acc + part
        conv_ref[:, lanes] = acc
    y = conv_ref[...]
    mu = jnp.mean(y, axis=-1, keepdims=True)
    yc = y - mu
    var = jnp.mean(yc * yc, axis=-1, keepdims=True)
    y = yc * lax.rsqrt(var + EPS) * lng_ref[...] + lnb_ref[...]
    o_ref[...] = (_silu(y) * gate_ref[...].astype(F32)).astype(o_ref.dtype)


def _conv_branch(y_glu, gate, dw_w, dw_b, ln_g, ln_b, t=128):
    s, c = y_glu.shape
    t = _tile(s, t)
    assert t % CONV_HALO == 0
    w = jnp.pad(dw_w, ((0, CONV_HALO - CONV_WIDTH), (0, 0)))
    row = pl.BlockSpec((1, c), lambda i: (0, 0))
    return pl.pallas_call(
        functools.partial(_conv_branch_kernel, t=t),
        out_shape=jax.ShapeDtypeStruct((s, c), BF16),
        grid=(s // t,),
        in_specs=[pl.BlockSpec((t, c), lambda i: (i, 0)),
                  pl.BlockSpec((CONV_HALO, c), lambda i: (jnp.maximum(i * (t // CONV_HALO) - 1, 0), 0)),
                  pl.BlockSpec((t, c), lambda i: (i, 0)),
                  pl.BlockSpec((CONV_HALO, c), lambda i: (0, 0)),
                  row, row, row],
        out_specs=pl.BlockSpec((t, c), lambda i: (i, 0)),
        scratch_shapes=[pltpu.VMEM((t + CONV_HALO, c), F32), pltpu.VMEM((t, c), F32)],
        compiler_params=_cparams("parallel"),
        name="conv_branch",
    )(y_glu, y_glu, gate, w, dw_b.reshape(1, c), ln_g.reshape(1, c), ln_b.reshape(1, c))


def _lru_kernel(cur_ref, prev_ref, gate_ref, cw_ref, cb_ref, wa_ref, ba_ref, wx_ref, bx_ref,
                sp_ref, o_ref, buf_ref, a_scr, b_scr, h_scr, carry_ref, *, t):
    i = pl.program_id(0)
    c = cur_ref.shape[1]
    prev = prev_ref[...]
    buf_ref[0:SUBLANES, :] = jnp.where(i > 0, prev, jnp.zeros_like(prev))
    buf_ref[SUBLANES:, :] = cur_ref[...]

    @pl.when(i == 0)
    def _():
        carry_ref[...] = jnp.zeros_like(carry_ref)

    xc = jnp.zeros((t, c), F32) + cb_ref[...]
    for k in range(LRU_CONV_WIDTH):
        off = SUBLANES - (LRU_CONV_WIDTH - 1) + k
        xc = xc + cw_ref[pl.ds(k, 1), :] * buf_ref[pl.ds(off, t), :]

    xb = xc.astype(BF16)
    nblk = wa_ref.shape[0]
    bd = c // nblk
    ra = jnp.concatenate(
        [jnp.dot(xb[:, g * bd:(g + 1) * bd], wa_ref[g], preferred_element_type=F32) for g in range(nblk)],
        axis=1)
    rx = jnp.concatenate(
        [jnp.dot(xb[:, g * bd:(g + 1) * bd], wx_ref[g], preferred_element_type=F32) for g in range(nblk)],
        axis=1)
    r = jax.nn.sigmoid(ra + ba_ref[...])
    gi = jax.nn.sigmoid(rx + bx_ref[...])
    a = jnp.exp((-LRU_C) * r * sp_ref[...])
    b = jnp.sqrt(1.0 - a * a) * (gi * xc)

    a3 = a.reshape(t // SUBLANES, SUBLANES, c)
    b3 = b.reshape(t // SUBLANES, SUBLANES, c)
    sub = lax.broadcasted_iota(jnp.int32, a3.shape, 1)
    for d in (1, 2, 4):
        keep = sub >= d
        a_sh = pltpu.roll(a3, d, 1)
        b_sh = pltpu.roll(b3, d, 1)
        b3 = jnp.where(keep, a3 * b_sh + b3, b3)
        a3 = jnp.where(keep, a3 * a_sh, a3)
    a_scr[...] = a3
    b_scr[...] = b3

    h_last = carry_ref[...]
    for j in range(t // SUBLANES):
        hj = a_scr[j] * h_last + b_scr[j]
        h_scr[j] = hj
        h_last = hj[SUBLANES - 1:SUBLANES, :]
    carry_ref[...] = h_last

    h = h_scr[...].reshape(t, c)
    o_ref[...] = (h * gate_ref[...].astype(F32)).astype(o_ref.dtype)


def _lru_branch(u, gate, cw, cb, w_a, b_a, w_x, b_x, lam, t=256):
    s, c = u.shape
    t = _tile(s, t)
    nblk, bd, _ = w_a.shape
    row = pl.BlockSpec((1, c), lambda i: (0, 0))
    wblk = pl.BlockSpec((nblk, bd, bd), lambda i: (0, 0, 0))
    cw_pad = jnp.pad(cw, ((0, SUBLANES - LRU_CONV_WIDTH), (0, 0)))
    return pl.pallas_call(
        functools.partial(_lru_kernel, t=t),
        out_shape=jax.ShapeDtypeStruct((s, c), BF16),
        grid=(s // t,),
        in_specs=[pl.BlockSpec((t, c), lambda i: (i, 0)),
                  pl.BlockSpec((SUBLANES, c), lambda i: (jnp.maximum(i * (t // SUBLANES) - 1, 0), 0)),
                  pl.BlockSpec((t, c), lambda i: (i, 0)),
                  pl.BlockSpec((SUBLANES, c), lambda i: (0, 0)),
                  row, wblk, row, wblk, row, row],
        out_specs=pl.BlockSpec((t, c), lambda i: (i, 0)),
        scratch_shapes=[pltpu.VMEM((t + SUBLANES, c), F32),
                        pltpu.VMEM((t // SUBLANES, SUBLANES, c), F32),
                        pltpu.VMEM((t // SUBLANES, SUBLANES, c), F32),
                        pltpu.VMEM((t // SUBLANES, SUBLANES, c), F32),
                        pltpu.VMEM((1, c), F32)],
        compiler_params=_cparams("arbitrary"),
        name="lru_branch",
    )(u, u, gate, cw_pad, cb.reshape(1, c), w_a, b_a.reshape(1, c), w_x, b_x.reshape(1, c),
      jax.nn.softplus(-lam).reshape(1, c))


HEAD_GROUP = 4
QK_PAD = 2 * LANES


def _q_kernel(a_ref, wn_ref, wr_ref, cos_ref, sin_ref, o_ref, *, scale):
    a = a_ref[...]
    qn = jnp.dot(a, wn_ref[...], preferred_element_type=F32) * scale
    qr = jnp.dot(a, wr_ref[...], preferred_element_type=F32)
    cos, sin = cos_ref[...], sin_ref[...]
    lane = lax.broadcasted_iota(jnp.int32, cos.shape, 1)
    for pair in range(HEAD_GROUP // 2):
        roped = _rope_lanes(qr[:, pair * LANES:(pair + 1) * LANES], cos, sin) * scale
        halves = (jnp.where(lane < QK_ROPE_DIM, roped, 0.0), jnp.where(lane >= QK_ROPE_DIM, roped, 0.0))
        for sub in range(2):
            h = 2 * pair + sub
            o_ref[h, :, 0:QK_NOPE_DIM] = qn[:, h * QK_NOPE_DIM:(h + 1) * QK_NOPE_DIM].astype(o_ref.dtype)
            o_ref[h, :, QK_NOPE_DIM:QK_PAD] = halves[sub].astype(o_ref.dtype)


def _q_proj(cqn, w_qn, w_qr, cos, sin, scale, tm=1024):
    m, k = cqn.shape
    nh = w_qn.shape[1] // QK_NOPE_DIM
    assert nh % HEAD_GROUP == 0
    tm = _tile(m, tm)
    return pl.pallas_call(
        functools.partial(_q_kernel, scale=scale),
        out_shape=jax.ShapeDtypeStruct((nh, m, QK_PAD), BF16),
        grid=(m // tm, nh // HEAD_GROUP),
        in_specs=[pl.BlockSpec((tm, k), lambda i, g: (i, 0)),
                  pl.BlockSpec((k, HEAD_GROUP * QK_NOPE_DIM), lambda i, g: (0, g)),
                  pl.BlockSpec((k, HEAD_GROUP * QK_ROPE_DIM), lambda i, g: (0, g)),
                  pl.BlockSpec((tm, LANES), lambda i, g: (i, 0)),
                  pl.BlockSpec((tm, LANES), lambda i, g: (i, 0))],
        out_specs=pl.BlockSpec((HEAD_GROUP, tm, QK_PAD), lambda i, g: (g, i, 0)),
        compiler_params=_cparams("parallel", "parallel"),
        name="q_proj",
    )(cqn, w_qn, w_qr, cos, sin)


def _kv_kernel(a_ref, wk_ref, wv_ref, kr_ref, k_ref, v_ref):
    a = a_ref[...]
    kn = jnp.dot(a, wk_ref[...], preferred_element_type=F32)
    v = jnp.dot(a, wv_ref[...], preferred_element_type=F32)
    kr = kr_ref[...]
    for h in range(HEAD_GROUP):
        k_ref[h, :, 0:QK_NOPE_DIM] = kn[:, h * QK_NOPE_DIM:(h + 1) * QK_NOPE_DIM].astype(k_ref.dtype)
        k_ref[h, :, QK_NOPE_DIM:QK_PAD] = kr
        v_ref[h] = v[:, h * V_DIM:(h + 1) * V_DIM].astype(v_ref.dtype)


def _kv_proj(ckvn, w_kn, w_v, kr, tm=1024):
    m, k = ckvn.shape
    nh = w_kn.shape[1] // QK_NOPE_DIM
    tm = _tile(m, tm)
    return pl.pallas_call(
        _kv_kernel,
        out_shape=(jax.ShapeDtypeStruct((nh, m, QK_PAD), BF16), jax.ShapeDtypeStruct((nh, m, V_DIM), BF16)),
        grid=(m // tm, nh // HEAD_GROUP),
        in_specs=[pl.BlockSpec((tm, k), lambda i, g: (i, 0)),
                  pl.BlockSpec((k, HEAD_GROUP * QK_NOPE_DIM), lambda i, g: (0, g)),
                  pl.BlockSpec((k, HEAD_GROUP * V_DIM), lambda i, g: (0, g)),
                  pl.BlockSpec((tm, LANES), lambda i, g: (i, 0))],
        out_specs=(pl.BlockSpec((HEAD_GROUP, tm, QK_PAD), lambda i, g: (g, i, 0)),
                   pl.BlockSpec((HEAD_GROUP, tm, V_DIM), lambda i, g: (g, i, 0))),
        compiler_params=_cparams("parallel", "parallel"),
        name="kv_proj",
    )(ckvn, w_kn, w_v, kr)


FLASH_HEADS = 2
FLASH_TQ = 1024
FLASH_TK = 1024
FLASH_TD = 512


def _flash_kernel(q_ref, k_ref, v_ref, g_ref, o_ref, m_scr, l_scr, acc_scr, *, tq, tk, td):
    qi = pl.program_id(1)
    nheads = q_ref.shape[0]
    m_scr[...] = jnp.full_like(m_scr, -jnp.inf)
    l_scr[...] = jnp.zeros_like(l_scr)
    acc_scr[...] = jnp.zeros_like(acc_scr)

    def step(off, klen, r0, diagonal):
        rows = slice(r0, tq)
        for h in range(nheads):
            q = q_ref[h, rows, :]
            k = k_ref[h, pl.ds(off, klen), :]
            v = v_ref[h, pl.ds(off, klen), :]
            s = lax.dot_general(q, k, (((1,), (1,)), ((), ())), preferred_element_type=F32)
            if diagonal:
                row = lax.broadcasted_iota(jnp.int32, s.shape, 0)
                col = lax.broadcasted_iota(jnp.int32, s.shape, 1)
                s = jnp.where(col <= row, s, jnp.finfo(F32).min)
            m_prev = m_scr[h, rows, :]
            m_new = jnp.maximum(m_prev, jnp.max(s, axis=1, keepdims=True))
            alpha = jnp.exp2(m_prev - m_new)
            p = jnp.exp2(s - jnp.tile(m_new, (1, klen // LANES)))
            psum = p[:, 0:LANES]
            for c in range(1, klen // LANES):
                psum = psum + p[:, c * LANES:(c + 1) * LANES]
            l_scr[h, rows, :] = alpha * l_scr[h, rows, :] + psum
            acc_scr[h, rows, :] = alpha * acc_scr[h, rows, :] + jnp.dot(
                p.astype(v.dtype), v, preferred_element_type=F32)
            m_scr[h, rows, :] = m_new

    def body(ki, carry):
        step(pl.multiple_of(ki * tk, tk), tk, 0, False)
        return carry

    lax.fori_loop(0, qi * (tq // tk), body, 0)
    q0 = pl.multiple_of(qi * tq, tq)
    for r in range(tq // td):
        step(q0 + r * td, td, r * td, True)
    for h in range(nheads):
        cols = slice(h * V_DIM, (h + 1) * V_DIM)
        o = acc_scr[h] / jnp.sum(l_scr[h], axis=1, keepdims=True)
        o_ref[:, cols] = (o * g_ref[:, cols].astype(F32)).astype(o_ref.dtype)


def _flash(q, k, v, gate):
    nh, s, _ = q.shape
    hp = FLASH_HEADS
    tq = _tile(s, FLASH_TQ)
    tk = _tile(tq, FLASH_TK)
    td = _tile(tk, FLASH_TD)
    assert nh % hp == 0
    return pl.pallas_call(
        functools.partial(_flash_kernel, tq=tq, tk=tk, td=td),
        out_shape=jax.ShapeDtypeStruct((s, nh * V_DIM), BF16),
        grid=(nh // hp, s // tq),
        in_specs=[pl.BlockSpec((hp, tq, QK_PAD), lambda h, i: (h, i, 0)),
                  pl.BlockSpec((hp, s, QK_PAD), lambda h, i: (h, 0, 0)),
                  pl.BlockSpec((hp, s, V_DIM), lambda h, i: (h, 0, 0)),
                  pl.BlockSpec((tq, hp * V_DIM), lambda h, i: (i, h))],
        out_specs=pl.BlockSpec((tq, hp * V_DIM), lambda h, i: (i, h)),
        scratch_shapes=[pltpu.VMEM((hp, tq, LANES), F32), pltpu.VMEM((hp, tq, LANES), F32),
                        pltpu.VMEM((hp, tq, V_DIM), F32)],
        compiler_params=_cparams("parallel", "parallel"),
        name="flash",
    )(q, k, v, gate)


def _merge_kernel(ya_ref, yb_ref, yc_ref, wa_ref, wb_ref, wc_ref, g0_ref, g1_ref, g2_ref, o_ref):
    pa = jnp.dot(ya_ref[...], wa_ref[...], preferred_element_type=F32)
    pb = jnp.dot(yb_ref[...], wb_ref[...], preferred_element_type=F32)
    pc = jnp.dot(yc_ref[...], wc_ref[...], preferred_element_type=F32)
    merged = (g0_ref[...].astype(F32) * pa + g1_ref[...].astype(F32) * pb) + g2_ref[...].astype(F32) * pc
    o_ref[...] = merged.astype(o_ref.dtype)


def _merge(ya, yb, yc, wa, wb, wc, gm, tm=512, tn=512):
    m = ya.shape[0]
    n = wa.shape[1]
    tm, tn = _tile(m, tm), _tile(n, tn)
    nb = n // tn
    lhs = lambda y: pl.BlockSpec((tm, y.shape[1]), lambda i, j: (i, 0))
    rhs = lambda w: pl.BlockSpec((w.shape[0], tn), lambda i, j: (0, j))
    gate = lambda b: pl.BlockSpec((tm, tn), lambda i, j: (i, j + b * nb))
    return pl.pallas_call(
        _merge_kernel,
        out_shape=jax.ShapeDtypeStruct((m, n), BF16),
        grid=(m // tm, nb),
        in_specs=[lhs(ya), lhs(yb), lhs(yc), rhs(wa), rhs(wb), rhs(wc), gate(0), gate(1), gate(2)],
        out_specs=pl.BlockSpec((tm, tn), lambda i, j: (i, j)),
        compiler_params=_cparams("parallel", "parallel"),
        name="merge",
    )(ya, yb, yc, wa, wb, wc, gm, gm, gm)


def kernel(x, positions, pre_norm_g, w_in, conv_dw_w, conv_dw_b, conv_ln_g, conv_ln_b, w_conv_proj, q_norm_g, w_uq, kv_norm_g, w_ukv, w_mla_proj, lru_conv_w, lru_conv_b, lru_w_a, lru_b_a, lru_w_x, lru_b_x, lru_lambda, w_lru_proj, w_out, post_norm_g):
    batch, seq, d_model = x.shape
    assert batch == 1
    depth = w_in.shape[0]
    nh = N_HEADS
    d_conv = conv_dw_w.shape[2]
    d_lru = lru_conv_w.shape[2]
    q_lora = w_uq.shape[1]
    kv_lora = w_ukv.shape[1]
    d_mla = nh * V_DIM
    qk_dim = QK_NOPE_DIM + QK_ROPE_DIM
    scale = qk_dim ** -0.5 * float(np.log2(np.e))
    splits = np.cumsum([0, 2 * d_conv, d_conv, q_lora, kv_lora, QK_ROPE_DIM, d_mla, d_lru, d_lru, 3 * d_model])
    (c_glu, c_gconv, c_cq, c_ckv, c_kr, c_gmla, c_ulru, c_glru, c_gm, c_end) = [int(v) for v in splits]
    assert c_end == w_in.shape[2]

    cos, sin = _rope_tables(positions[0])
    xs = x[0]
    h = _prenorm(xs, pre_norm_g[0])

    for l in range(depth):
        wl = w_in[l]
        cast = lambda w: w.astype(BF16)
        w_cq = cast(wl[:, c_cq:c_ckv])
        w_kr = wl[:, c_kr:c_gmla]
        w_ckvr = cast(jnp.concatenate([wl[:, c_ckv:c_kr], w_kr, w_kr], axis=1))
        uq = w_uq[l].reshape(q_lora, nh, qk_dim)
        w_qn = cast(uq[:, :, :QK_NOPE_DIM].reshape(q_lora, nh * QK_NOPE_DIM))
        w_qr = cast(uq[:, :, QK_NOPE_DIM:].reshape(q_lora, nh * QK_ROPE_DIM))
        ukv = w_ukv[l].reshape(kv_lora, nh, QK_NOPE_DIM + V_DIM)
        w_kn = cast(ukv[:, :, :QK_NOPE_DIM].reshape(kv_lora, nh * QK_NOPE_DIM))
        w_v = cast(ukv[:, :, QK_NOPE_DIM:].reshape(kv_lora, nh * V_DIM))

        y_glu = _proj(h, w_in, l, c_glu, d_conv, F32, act="glu", tn=256, name="glu_proj")
        cqn = _cq_proj(h, w_cq, q_norm_g[l])
        ckvn, kr = _ckvr_proj(h, w_ckvr, kv_norm_g[l], cos, sin)
        u_lru = _proj(h, w_in, l, c_ulru, d_lru, F32, name="ulru_proj")
        g_conv = _proj(h, w_in, l, c_gconv, d_conv, BF16, act="silu", name="conv_gate_proj")
        g_mla = _proj(h, w_in, l, c_gmla, d_mla, BF16, act="silu", name="mla_gate_proj")
        g_lru = _proj(h, w_in, l, c_glru, d_lru, BF16, act="silu", name="lru_gate_proj")
        gm = _proj(h, w_in, l, c_gm, 3 * d_model, BF16, act="sigmoid", name="merge_gate_proj")

        y_a = _conv_branch(y_glu, g_conv, conv_dw_w[l], conv_dw_b[l], conv_ln_g[l], conv_ln_b[l])
        q = _q_proj(cqn, w_qn, w_qr, cos, sin, scale)
        k, v = _kv_proj(ckvn, w_kn, w_v, kr)
        y_b = _flash(q, k, v, g_mla)
        y_c = _lru_branch(u_lru, g_lru, lru_conv_w[l], lru_conv_b[l], cast(lru_w_a[l]),
                          lru_b_a[l], cast(lru_w_x[l]), lru_b_x[l], lru_lambda[l])

        merged = _merge(y_a, y_b, y_c, cast(w_conv_proj[l]), cast(w_mla_proj[l]), cast(w_lru_proj[l]), gm)
        out = _proj(merged, w_out, l, 0, d_model, F32, name="out_proj")
        g_next = pre_norm_g[l + 1] if l + 1 < depth else pre_norm_g[l]
        xs, h = _post(xs, out, post_norm_g[l], g_next)
    return xs[None]
```

```python
import functools

import jax
import jax.numpy as jnp
import numpy as np
from jax import lax
from jax.experimental import pallas as pl
from jax.experimental.pallas import tpu as pltpu

N_HEADS = 32
QK_NOPE_DIM = 128
QK_ROPE_DIM = 64
V_DIM = 128
CONV_WIDTH = 31
LRU_CONV_WIDTH = 4
LRU_BLOCKS = 16
LRU_C = 8.0
ROPE_THETA = 10000.0
EPS = 1e-6

LANES = 128
SUBLANES = 8
CONV_HALO = 32
VMEM_LIMIT_BYTES = 52 * 1024 * 1024

F32 = jnp.float32
BF16 = jnp.bfloat16


def _cparams(*sem):
    return pltpu.CompilerParams(dimension_semantics=sem, vmem_limit_bytes=VMEM_LIMIT_BYTES)


def _tile(dim, pref):
    t = min(dim, pref)
    assert dim % t == 0, (dim, t)
    return t


def _rms(x, g):
    return x * lax.rsqrt(jnp.mean(x * x, axis=-1, keepdims=True) + EPS) * g


def _silu(x):
    return x * jax.nn.sigmoid(x)


def _rope_lanes(x, cos, sin_signed):
    lane = lax.broadcasted_iota(jnp.int32, x.shape, 1)
    first_half = (lane % QK_ROPE_DIM) < (QK_ROPE_DIM // 2)
    partner = jnp.where(first_half,
                        pltpu.roll(x, LANES - QK_ROPE_DIM // 2, 1),
                        pltpu.roll(x, QK_ROPE_DIM // 2, 1))
    return x * cos + partner * sin_signed


def _rope_table_kernel(pos_ref, freq_ref, sign_ref, cos_ref, sin_ref):
    ang = pos_ref[...].astype(F32) * freq_ref[...]
    cos_ref[...] = jnp.cos(ang)
    sin_ref[...] = jnp.sin(ang) * sign_ref[...]


def _rope_tables(positions):
    s = positions.shape[0]
    half = QK_ROPE_DIM // 2
    inv_freq = ROPE_THETA ** (-jnp.arange(0, QK_ROPE_DIM, 2, dtype=F32) / QK_ROPE_DIM)
    freq = jnp.tile(inv_freq, LANES // half)[None, :]
    sign = jnp.tile(jnp.concatenate([-jnp.ones((half,), F32), jnp.ones((half,), F32)]),
                    LANES // QK_ROPE_DIM)[None, :]
    t = _tile(s, 512)
    row = pl.BlockSpec((1, LANES), lambda i: (0, 0))
    out = pl.BlockSpec((t, LANES), lambda i: (i, 0))
    return pl.pallas_call(
        _rope_table_kernel,
        out_shape=(jax.ShapeDtypeStruct((s, LANES), F32),) * 2,
        grid=(s // t,),
        in_specs=[pl.BlockSpec((t, 1), lambda i: (i, 0)), row, row],
        out_specs=(out, out),
        compiler_params=_cparams("parallel"),
        name="rope_tables",
    )(positions.reshape(s, 1), freq, sign)


def _prenorm_kernel(x_ref, g_ref, h_ref):
    h_ref[...] = _rms(x_ref[...], g_ref[...]).astype(h_ref.dtype)


def _prenorm(x, g):
    s, d = x.shape
    t = _tile(s, 256)
    return pl.pallas_call(
        _prenorm_kernel,
        out_shape=jax.ShapeDtypeStruct((s, d), BF16),
        grid=(s // t,),
        in_specs=[pl.BlockSpec((t, d), lambda i: (i, 0)), pl.BlockSpec((1, d), lambda i: (0, 0))],
        out_specs=pl.BlockSpec((t, d), lambda i: (i, 0)),
        compiler_params=_cparams("parallel"),
        name="prenorm",
    )(x, g.reshape(1, d))


def _post_kernel(x_ref, o_ref, gpost_ref, gpre_ref, xn_ref, h_ref):
    xn = x_ref[...] + _rms(o_ref[...], gpost_ref[...])
    xn_ref[...] = xn
    h_ref[...] = _rms(xn, gpre_ref[...]).astype(h_ref.dtype)


def _post(x, out, g_post, g_pre_next):
    s, d = x.shape
    t = _tile(s, 256)
    tile = pl.BlockSpec((t, d), lambda i: (i, 0))
    row = pl.BlockSpec((1, d), lambda i: (0, 0))
    return pl.pallas_call(
        _post_kernel,
        out_shape=(jax.ShapeDtypeStruct((s, d), F32), jax.ShapeDtypeStruct((s, d), BF16)),
        grid=(s // t,),
        in_specs=[tile, tile, row, row],
        out_specs=(tile, tile),
        compiler_params=_cparams("parallel"),
        name="post",
    )(x, out, g_post.reshape(1, d), g_pre_next.reshape(1, d))


NT_DIMS = (((1,), (1,)), ((), ()))


def _dot_nt(a, w):
    return lax.dot_general(a, w, NT_DIMS, preferred_element_type=F32)


def _proj_kernel(a_ref, *refs, act, n_w, w_is_nk):
    w_refs, o_ref, w_scr = refs[:n_w], refs[n_w], refs[n_w + 1:]

    @pl.when(pl.program_id(1) == 0)
    def _():
        for w_ref, scr in zip(w_refs, w_scr):
            scr[...] = w_ref[0].astype(BF16)

    a = a_ref[...]
    dot = _dot_nt if w_is_nk else functools.partial(jnp.dot, preferred_element_type=F32)
    acc = dot(a, w_scr[0][...])
    if act == "glu":
        acc = acc * jax.nn.sigmoid(dot(a, w_scr[1][...]))
    elif act == "silu":
        acc = _silu(acc)
    elif act == "sigmoid":
        acc = jax.nn.sigmoid(acc)
    o_ref[...] = acc.astype(o_ref.dtype)


def _proj(a, w_all, layer, col0, ncols, out_dtype, act=None, w_is_nk=False, tm=1024, tn=512, name="proj"):
    m, k = a.shape
    tm, tn = _tile(m, tm), _tile(ncols, tn)
    n_w = 2 if act == "glu" else 1
    starts = [col0 + i * ncols for i in range(n_w)]
    if w_is_nk:
        assert all(c % SUBLANES == 0 for c in starts)
        w_specs = [pl.BlockSpec((pl.Element(1), pl.Element(tn), pl.Element(k)),
                                lambda j, r, c=c: (layer, pl.multiple_of(c + j * tn, SUBLANES), 0))
                   for c in starts]
        w_tile = (tn, k)
    else:
        assert all(c % tn == 0 for c in starts)
        w_specs = [pl.BlockSpec((1, k, tn), lambda j, r, c=c: (layer, 0, c // tn + j)) for c in starts]
        w_tile = (k, tn)
    return pl.pallas_call(
        functools.partial(_proj_kernel, act=act, n_w=n_w, w_is_nk=w_is_nk),
        out_shape=jax.ShapeDtypeStruct((m, ncols), out_dtype),
        grid=(ncols // tn, m // tm),
        in_specs=[pl.BlockSpec((tm, k), lambda j, r: (r, 0))] + w_specs,
        out_specs=pl.BlockSpec((tm, tn), lambda j, r: (r, j)),
        scratch_shapes=[pltpu.VMEM(w_tile, BF16)] * n_w,
        compiler_params=_cparams("parallel", "arbitrary"),
        name=name,
    )(a, *([w_all] * n_w))


def _cq_kernel(a_ref, w_ref, g_ref, o_ref):
    o_ref[...] = _rms(_dot_nt(a_ref[...], w_ref[...]), g_ref[...]).astype(o_ref.dtype)


def _cq_proj(a, w_nk, g, tm=512):
    m, k = a.shape
    n = w_nk.shape[0]
    tm = _tile(m, tm)
    return pl.pallas_call(
        _cq_kernel,
        out_shape=jax.ShapeDtypeStruct((m, n), BF16),
        grid=(m // tm,),
        in_specs=[pl.BlockSpec((tm, k), lambda i: (i, 0)),
                  pl.BlockSpec((n, k), lambda i: (0, 0)),
                  pl.BlockSpec((1, n), lambda i: (0, 0))],
        out_specs=pl.BlockSpec((tm, n), lambda i: (i, 0)),
        compiler_params=_cparams("parallel"),
        name="cq_proj",
    )(a, w_nk, g.reshape(1, n))


def _ckvr_kernel(a_ref, w_ref, g_ref, cos_ref, sin_ref, ckv_ref, kr_ref, *, kv_lora):
    acc = _dot_nt(a_ref[...], w_ref[...])
    ckv_ref[...] = _rms(acc[:, :kv_lora], g_ref[...]).astype(ckv_ref.dtype)
    kr_ref[...] = _rope_lanes(acc[:, kv_lora:], cos_ref[...], sin_ref[...]).astype(kr_ref.dtype)


def _ckvr_proj(a, w_nk, g, cos, sin, tm=512):
    m, k = a.shape
    n = w_nk.shape[0]
    kv_lora = n - LANES
    tm = _tile(m, tm)
    return pl.pallas_call(
        functools.partial(_ckvr_kernel, kv_lora=kv_lora),
        out_shape=(jax.ShapeDtypeStruct((m, kv_lora), BF16), jax.ShapeDtypeStruct((m, LANES), BF16)),
        grid=(m // tm,),
        in_specs=[pl.BlockSpec((tm, k), lambda i: (i, 0)),
                  pl.BlockSpec((n, k), lambda i: (0, 0)),
                  pl.BlockSpec((1, kv_lora), lambda i: (0, 0)),
                  pl.BlockSpec((tm, LANES), lambda i: (i, 0)),
                  pl.BlockSpec((tm, LANES), lambda i: (i, 0))],
        out_specs=(pl.BlockSpec((tm, kv_lora), lambda i: (i, 0)),
                   pl.BlockSpec((tm, LANES), lambda i: (i, 0))),
        compiler_params=_cparams("parallel"),
        name="ckvr_proj",
    )(a, w_nk, g.reshape(1, kv_lora), cos, sin)


def _conv_branch_kernel(cur_ref, prev_ref, gate_ref, w_ref, b_ref, lng_ref, lnb_ref, o_ref,
                        buf_ref, conv_ref, *, t):
    i = pl.program_id(0)
    c = cur_ref.shape[1]
    prev = prev_ref[...]
    buf_ref[0:CONV_HALO, :] = jnp.where(i > 0, prev, jnp.zeros_like(prev))
    buf_ref[CONV_HALO:, :] = cur_ref[...]
    aligned = CONV_HALO - SUBLANES
    for c0 in range(0, c, LANES):
        lanes = slice(c0, c0 + LANES)
        acc = jnp.zeros((t, LANES), F32) + b_ref[:, lanes]
        for r in range(SUBLANES):
            z = buf_ref[pl.ds(SUBLANES - r, t + aligned), lanes]
            part = None
            for q in range(CONV_HALO // SUBLANES):
                d = SUBLANES * q + r
                if d >= CONV_WIDTH:
                    continue
                off = aligned - SUBLANES * q
                term = w_ref[pl.ds(CONV_WIDTH - 1 - d, 1), lanes] * z[off:off + t, :]
                part = term if part is None else part + term
            acc = acc + part
        conv_ref[:, lanes] = acc
    y = conv_ref[...]
    mu = jnp.mean(y, axis=-1, keepdims=True)
    yc = y - mu
    var = jnp.mean(yc * yc, axis=-1, keepdims=True)
    y = yc * lax.rsqrt(var + EPS) * lng_ref[...] + lnb_ref[...]
    o_ref[...] = (_silu(y) * gate_ref[...].astype(F32)).astype(o_ref.dtype)


def _conv_branch(y_glu, gate, dw_w, dw_b, ln_g, ln_b, t=128):
    s, c = y_glu.shape
    t = _tile(s, t)
    assert t % CONV_HALO == 0
    w = jnp.pad(dw_w, ((0, CONV_HALO - CONV_WIDTH), (0, 0)))
    row = pl.BlockSpec((1, c), lambda i: (0, 0))
    return pl.pallas_call(
        functools.partial(_conv_branch_kernel, t=t),
        out_shape=jax.ShapeDtypeStruct((s, c), BF16),
        grid=(s // t,),
        in_specs=[pl.BlockSpec((t, c), lambda i: (i, 0)),
                  pl.BlockSpec((CONV_HALO, c), lambda i: (jnp.maximum(i * (t // CONV_HALO) - 1, 0), 0)),
                  pl.BlockSpec((t, c), lambda i: (i, 0)),
                  pl.BlockSpec((CONV_HALO, c), lambda i: (0, 0)),
                  row, row, row],
        out_specs=pl.BlockSpec((t, c), lambda i: (i, 0)),
        scratch_shapes=[pltpu.VMEM((t + CONV_HALO, c), F32), pltpu.VMEM((t, c), F32)],
        compiler_params=_cparams("parallel"),
        name="conv_branch",
    )(y_glu, y_glu, gate, w, dw_b.reshape(1, c), ln_g.reshape(1, c), ln_b.reshape(1, c))


def _lru_kernel(cur_ref, prev_ref, gate_ref, cw_ref, cb_ref, wa_ref, ba_ref, wx_ref, bx_ref,
                sp_ref, o_ref, buf_ref, a_scr, b_scr, h_scr, carry_ref, *, t):
    i = pl.program_id(0)
    c = cur_ref.shape[1]
    prev = prev_ref[...]
    buf_ref[0:SUBLANES, :] = jnp.where(i > 0, prev, jnp.zeros_like(prev))
    buf_ref[SUBLANES:, :] = cur_ref[...]

    @pl.when(i == 0)
    def _():
        carry_ref[...] = jnp.zeros_like(carry_ref)

    xc = jnp.zeros((t, c), F32) + cb_ref[...]
    for k in range(LRU_CONV_WIDTH):
        off = SUBLANES - (LRU_CONV_WIDTH - 1) + k
        xc = xc + cw_ref[pl.ds(k, 1), :] * buf_ref[pl.ds(off, t), :]

    xb = xc.astype(BF16)
    nblk = wa_ref.shape[0]
    bd = c // nblk
    ra = jnp.concatenate(
        [jnp.dot(xb[:, g * bd:(g + 1) * bd], wa_ref[g], preferred_element_type=F32) for g in range(nblk)],
        axis=1)
    rx = jnp.concatenate(
        [jnp.dot(xb[:, g * bd:(g + 1) * bd], wx_ref[g], preferred_element_type=F32) for g in range(nblk)],
        axis=1)
    r = jax.nn.sigmoid(ra + ba_ref[...])
    gi = jax.nn.sigmoid(rx + bx_ref[...])
    a = jnp.exp((-LRU_C) * r * sp_ref[...])
    v = 1.0 - a * a
    b = jnp.where(v > 0.0, v * lax.rsqrt(v), 0.0) * (gi * xc)

    a3 = a.reshape(t // SUBLANES, SUBLANES, c)
    b3 = b.reshape(t // SUBLANES, SUBLANES, c)
    sub = lax.broadcasted_iota(jnp.int32, a3.shape, 1)
    for d in (1, 2, 4):
        keep = sub >= d
        a_sh = pltpu.roll(a3, d, 1)
        b_sh = pltpu.roll(b3, d, 1)
        b3 = jnp.where(keep, a3 * b_sh + b3, b3)
        a3 = jnp.where(keep, a3 * a_sh, a3)
    a_scr[...] = a3
    b_scr[...] = b3

    h_last = carry_ref[...]
    for j in range(t // SUBLANES):
        hj = a_scr[j] * h_last + b_scr[j]
        h_scr[j] = hj
        h_last = hj[SUBLANES - 1:SUBLANES, :]
    carry_ref[...] = h_last

    h = h_scr[...].reshape(t, c)
    o_ref[...] = (h * gate_ref[...].astype(F32)).astype(o_ref.dtype)


def _lru_branch(u, gate, cw, cb, w_a, b_a, w_x, b_x, lam, t=256):
    s, c = u.shape
    t = _tile(s, t)
    nblk, bd, _ = w_a.shape
    row = pl.BlockSpec((1, c), lambda i: (0, 0))
    wblk = pl.BlockSpec((nblk, bd, bd), lambda i: (0, 0, 0))
    cw_pad = jnp.pad(cw, ((0, SUBLANES - LRU_CONV_WIDTH), (0, 0)))
    return pl.pallas_call(
        functools.partial(_lru_kernel, t=t),
        out_shape=jax.ShapeDtypeStruct((s, c), BF16),
        grid=(s // t,),
        in_specs=[pl.BlockSpec((t, c), lambda i: (i, 0)),
                  pl.BlockSpec((SUBLANES, c), lambda i: (jnp.maximum(i * (t // SUBLANES) - 1, 0), 0)),
                  pl.BlockSpec((t, c), lambda i: (i, 0)),
                  pl.BlockSpec((SUBLANES, c), lambda i: (0, 0)),
                  row, wblk, row, wblk, row, row],
        out_specs=pl.BlockSpec((t, c), lambda i: (i, 0)),
        scratch_shapes=[pltpu.VMEM((t + SUBLANES, c), F32),
                        pltpu.VMEM((t // SUBLANES, SUBLANES, c), F32),
                        pltpu.VMEM((t // SUBLANES, SUBLANES, c), F32),
                        pltpu.VMEM((t // SUBLANES, SUBLANES, c), F32),
                        pltpu.VMEM((1, c), F32)],
        compiler_params=_cparams("arbitrary"),
        name="lru_branch",
    )(u, u, gate, cw_pad, cb.reshape(1, c), w_a, b_a.reshape(1, c), w_x, b_x.reshape(1, c),
      jax.nn.softplus(-lam).reshape(1, c))


HEAD_GROUP = 4
QK_PAD = 2 * LANES


def _q_kernel(a_ref, wn_ref, wr_ref, cos_ref, sin_ref, o_ref, *, scale):
    a = a_ref[...]
    qn = jnp.dot(a, wn_ref[...], preferred_element_type=F32) * scale
    qr = jnp.dot(a, wr_ref[...], preferred_element_type=F32)
    cos, sin = cos_ref[...], sin_ref[...]
    lane = lax.broadcasted_iota(jnp.int32, cos.shape, 1)
    for pair in range(HEAD_GROUP // 2):
        roped = _rope_lanes(qr[:, pair * LANES:(pair + 1) * LANES], cos, sin) * scale
        halves = (jnp.where(lane < QK_ROPE_DIM, roped, 0.0), jnp.where(lane >= QK_ROPE_DIM, roped, 0.0))
        for sub in range(2):
            h = 2 * pair + sub
            o_ref[h, :, 0:QK_NOPE_DIM] = qn[:, h * QK_NOPE_DIM:(h + 1) * QK_NOPE_DIM].astype(o_ref.dtype)
            o_ref[h, :, QK_NOPE_DIM:QK_PAD] = halves[sub].astype(o_ref.dtype)


def _q_proj(cqn, w_qn, w_qr, cos, sin, scale, tm=1024):
    m, k = cqn.shape
    nh = w_qn.shape[1] // QK_NOPE_DIM
    assert nh % HEAD_GROUP == 0
    tm = _tile(m, tm)
    return pl.pallas_call(
        functools.partial(_q_kernel, scale=scale),
        out_shape=jax.ShapeDtypeStruct((nh, m, QK_PAD), BF16),
        grid=(m // tm, nh // HEAD_GROUP),
        in_specs=[pl.BlockSpec((tm, k), lambda i, g: (i, 0)),
                  pl.BlockSpec((k, HEAD_GROUP * QK_NOPE_DIM), lambda i, g: (0, g)),
                  pl.BlockSpec((k, HEAD_GROUP * QK_ROPE_DIM), lambda i, g: (0, g)),
                  pl.BlockSpec((tm, LANES), lambda i, g: (i, 0)),
                  pl.BlockSpec((tm, LANES), lambda i, g: (i, 0))],
        out_specs=pl.BlockSpec((HEAD_GROUP, tm, QK_PAD), lambda i, g: (g, i, 0)),
        compiler_params=_cparams("parallel", "parallel"),
        name="q_proj",
    )(cqn, w_qn, w_qr, cos, sin)


def _kv_kernel(a_ref, wk_ref, wv_ref, kr_ref, k_ref, v_ref):
    a = a_ref[...]
    kn = jnp.dot(a, wk_ref[...], preferred_element_type=F32)
    v = jnp.dot(a, wv_ref[...], preferred_element_type=F32)
    kr = kr_ref[...]
    for h in range(HEAD_GROUP):
        k_ref[h, :, 0:QK_NOPE_DIM] = kn[:, h * QK_NOPE_DIM:(h + 1) * QK_NOPE_DIM].astype(k_ref.dtype)
        k_ref[h, :, QK_NOPE_DIM:QK_PAD] = kr
        v_ref[h] = v[:, h * V_DIM:(h + 1) * V_DIM].astype(v_ref.dtype)


def _kv_proj(ckvn, w_kn, w_v, kr, tm=1024):
    m, k = ckvn.shape
    nh = w_kn.shape[1] // QK_NOPE_DIM
    tm = _tile(m, tm)
    return pl.pallas_call(
        _kv_kernel,
        out_shape=(jax.ShapeDtypeStruct((nh, m, QK_PAD), BF16), jax.ShapeDtypeStruct((nh, m, V_DIM), BF16)),
        grid=(m // tm, nh // HEAD_GROUP),
        in_specs=[pl.BlockSpec((tm, k), lambda i, g: (i, 0)),
                  pl.BlockSpec((k, HEAD_GROUP * QK_NOPE_DIM), lambda i, g: (0, g)),
                  pl.BlockSpec((k, HEAD_GROUP * V_DIM), lambda i, g: (0, g)),
                  pl.BlockSpec((tm, LANES), lambda i, g: (i, 0))],
        out_specs=(pl.BlockSpec((HEAD_GROUP, tm, QK_PAD), lambda i, g: (g, i, 0)),
                   pl.BlockSpec((HEAD_GROUP, tm, V_DIM), lambda i, g: (g, i, 0))),
        compiler_params=_cparams("parallel", "parallel"),
        name="kv_proj",
    )(ckvn, w_kn, w_v, kr)


FLASH_HEADS = 2
FLASH_TQ = 1024
FLASH_TK = 1024
FLASH_TD = 512


def _flash_kernel(q_ref, k_ref, v_ref, g_ref, o_ref, m_scr, l_scr, acc_scr, *, tq, tk, td):
    qi = pl.program_id(1)
    nheads = q_ref.shape[0]
    m_scr[...] = jnp.full_like(m_scr, -jnp.inf)
    l_scr[...] = jnp.zeros_like(l_scr)
    acc_scr[...] = jnp.zeros_like(acc_scr)

    def step(off, klen, r0, diagonal):
        rows = slice(r0, tq)
        for h in range(nheads):
            q = q_ref[h, rows, :]
            k = k_ref[h, pl.ds(off, klen), :]
            v = v_ref[h, pl.ds(off, klen), :]
            s = lax.dot_general(q, k, (((1,), (1,)), ((), ())), preferred_element_type=F32)
            if diagonal:
                row = lax.broadcasted_iota(jnp.int32, s.shape, 0)
                col = lax.broadcasted_iota(jnp.int32, s.shape, 1)
                s = jnp.where(col <= row, s, jnp.finfo(F32).min)
            m_prev = m_scr[h, rows, :]
            m_new = jnp.maximum(m_prev, jnp.max(s, axis=1, keepdims=True))
            alpha = jnp.exp2(m_prev - m_new)
            p = jnp.exp2(s - jnp.tile(m_new, (1, klen // LANES)))
            psum = p[:, 0:LANES]
            for c in range(1, klen // LANES):
                psum = psum + p[:, c * LANES:(c + 1) * LANES]
            l_scr[h, rows, :] = alpha * l_scr[h, rows, :] + psum
            acc_scr[h, rows, :] = alpha * acc_scr[h, rows, :] + jnp.dot(
                p.astype(v.dtype), v, preferred_element_type=F32)
            m_scr[h, rows, :] = m_new

    def body(ki, carry):
        step(pl.multiple_of(ki * tk, tk), tk, 0, False)
        return carry

    lax.fori_loop(0, qi * (tq // tk), body, 0)
    q0 = pl.multiple_of(qi * tq, tq)
    for r in range(tq // td):
        step(q0 + r * td, td, r * td, True)
    for h in range(nheads):
        cols = slice(h * V_DIM, (h + 1) * V_DIM)
        o = acc_scr[h] / jnp.sum(l_scr[h], axis=1, keepdims=True)
        o_ref[:, cols] = (o * g_ref[:, cols].astype(F32)).astype(o_ref.dtype)


def _flash(q, k, v, gate):
    nh, s, _ = q.shape
    hp = FLASH_HEADS
    tq = _tile(s, FLASH_TQ)
    tk = _tile(tq, FLASH_TK)
    td = _tile(tk, FLASH_TD)
    assert nh % hp == 0
    return pl.pallas_call(
        functools.partial(_flash_kernel, tq=tq, tk=tk, td=td),
        out_shape=jax.ShapeDtypeStruct((s, nh * V_DIM), BF16),
        grid=(nh // hp, s // tq),
        in_specs=[pl.BlockSpec((hp, tq, QK_PAD), lambda h, i: (h, i, 0)),
                  pl.BlockSpec((hp, s, QK_PAD), lambda h, i: (h, 0, 0)),
                  pl.BlockSpec((hp, s, V_DIM), lambda h, i: (h, 0, 0)),
                  pl.BlockSpec((tq, hp * V_DIM), lambda h, i: (i, h))],
        out_specs=pl.BlockSpec((tq, hp * V_DIM), lambda h, i: (i, h)),
        scratch_shapes=[pltpu.VMEM((hp, tq, LANES), F32), pltpu.VMEM((hp, tq, LANES), F32),
                        pltpu.VMEM((hp, tq, V_DIM), F32)],
        compiler_params=_cparams("parallel", "parallel"),
        name="flash",
    )(q, k, v, gate)


def _merge_kernel(ya_ref, yb_ref, yc_ref, wa_ref, wb_ref, wc_ref, g0_ref, g1_ref, g2_ref, o_ref):
    pa = jnp.dot(ya_ref[...], wa_ref[...], preferred_element_type=F32)
    pb = jnp.dot(yb_ref[...], wb_ref[...], preferred_element_type=F32)
    pc = jnp.dot(yc_ref[...], wc_ref[...], preferred_element_type=F32)
    merged = (g0_ref[...].astype(F32) * pa + g1_ref[...].astype(F32) * pb) + g2_ref[...].astype(F32) * pc
    o_ref[...] = merged.astype(o_ref.dtype)


def _merge(ya, yb, yc, wa, wb, wc, gm, tm=512, tn=512):
    m = ya.shape[0]
    n = wa.shape[1]
    tm, tn = _tile(m, tm), _tile(n, tn)
    nb = n // tn
    lhs = lambda y: pl.BlockSpec((tm, y.shape[1]), lambda i, j: (i, 0))
    rhs = lambda w: pl.BlockSpec((w.shape[0], tn), lambda i, j: (0, j))
    gate = lambda b: pl.BlockSpec((tm, tn), lambda i, j: (i, j + b * nb))
    return pl.pallas_call(
        _merge_kernel,
        out_shape=jax.ShapeDtypeStruct((m, n), BF16),
        grid=(m // tm, nb),
        in_specs=[lhs(ya), lhs(yb), lhs(yc), rhs(wa), rhs(wb), rhs(wc), gate(0), gate(1), gate(2)],
        out_specs=pl.BlockSpec((tm, tn), lambda i, j: (i, j)),
        compiler_params=_cparams("parallel", "parallel"),
        name="merge",
    )(ya, yb, yc, wa, wb, wc, gm, gm, gm)


def kernel(x, positions, pre_norm_g, w_in, conv_dw_w, conv_dw_b, conv_ln_g, conv_ln_b, w_conv_proj, q_norm_g, w_uq, kv_norm_g, w_ukv, w_mla_proj, lru_conv_w, lru_conv_b, lru_w_a, lru_b_a, lru_w_x, lru_b_x, lru_lambda, w_lru_proj, w_out, post_norm_g):
    batch, seq, d_model = x.shape
    assert batch == 1
    depth = w_in.shape[0]
    nh = N_HEADS
    d_conv = conv_dw_w.shape[2]
    d_lru = lru_conv_w.shape[2]
    q_lora = w_uq.shape[1]
    kv_lora = w_ukv.shape[1]
    d_mla = nh * V_DIM
    qk_dim = QK_NOPE_DIM + QK_ROPE_DIM
    scale = qk_dim ** -0.5 * float(np.log2(np.e))
    splits = np.cumsum([0, 2 * d_conv, d_conv, q_lora, kv_lora, QK_ROPE_DIM, d_mla, d_lru, d_lru, 3 * d_model])
    (c_glu, c_gconv, c_cq, c_ckv, c_kr, c_gmla, c_ulru, c_glru, c_gm, c_end) = [int(v) for v in splits]
    assert c_end == w_in.shape[2]

    cos, sin = _rope_tables(positions[0])
    xs = x[0]
    h = _prenorm(xs, pre_norm_g[0])

    w_in_nk = jnp.swapaxes(w_in, 1, 2)
    inproj = functools.partial(_proj, w_is_nk=True)

    for l in range(depth):
        cast = lambda w: w.astype(BF16)
        w_lat = lax.optimization_barrier(w_in_nk[l, c_cq:c_gmla])
        w_cq = cast(w_lat[:q_lora])
        w_ckvr = cast(jnp.concatenate([w_lat[q_lora:], w_lat[q_lora + kv_lora:]], axis=0))
        uq = w_uq[l].reshape(q_lora, nh, qk_dim)
        w_qn = cast(uq[:, :, :QK_NOPE_DIM].reshape(q_lora, nh * QK_NOPE_DIM))
        w_qr = cast(uq[:, :, QK_NOPE_DIM:].reshape(q_lora, nh * QK_ROPE_DIM))
        ukv = w_ukv[l].reshape(kv_lora, nh, QK_NOPE_DIM + V_DIM)
        w_kn = cast(ukv[:, :, :QK_NOPE_DIM].reshape(kv_lora, nh * QK_NOPE_DIM))
        w_v = cast(ukv[:, :, QK_NOPE_DIM:].reshape(kv_lora, nh * V_DIM))

        y_glu = inproj(h, w_in_nk, l, c_glu, d_conv, F32, act="glu", tn=256, name="glu_proj")
        cqn = _cq_proj(h, w_cq, q_norm_g[l])
        ckvn, kr = _ckvr_proj(h, w_ckvr, kv_norm_g[l], cos, sin)
        u_lru = inproj(h, w_in_nk, l, c_ulru, d_lru, F32, name="ulru_proj")
        g_conv = inproj(h, w_in_nk, l, c_gconv, d_conv, BF16, act="silu", name="conv_gate_proj")
        g_mla = inproj(h, w_in_nk, l, c_gmla, d_mla, BF16, act="silu", name="mla_gate_proj")
        g_lru = inproj(h, w_in_nk, l, c_glru, d_lru, BF16, act="silu", name="lru_gate_proj")
        gm = inproj(h, w_in_nk, l, c_gm, 3 * d_model, BF16, act="sigmoid", name="merge_gate_proj")

        y_a = _conv_branch(y_glu, g_conv, conv_dw_w[l], conv_dw_b[l], conv_ln_g[l], conv_ln_b[l])
        q = _q_proj(cqn, w_qn, w_qr, cos, sin, scale)
        k, v = _kv_proj(ckvn, w_kn, w_v, kr)
        y_b = _flash(q, k, v, g_mla)
        y_c = _lru_branch(u_lru, g_lru, lru_conv_w[l], lru_conv_b[l], cast(lru_w_a[l]),
                          lru_b_a[l], cast(lru_w_x[l]), lru_b_x[l], lru_lambda[l])

        merged = _merge(y_a, y_b, y_c, cast(w_conv_proj[l]), cast(w_mla_proj[l]), cast(w_lru_proj[l]), gm)
        out = _proj(merged, w_out, l, 0, d_model, F32, name="out_proj")
        g_next = pre_norm_g[l + 1] if l + 1 < depth else pre_norm_g[l]
        xs, h = _post(xs, out, post_norm_g[l], g_next)
    return xs[None]
```

```python
import functools

import jax
import jax.numpy as jnp
import numpy as np
from jax import lax
from jax.experimental import pallas as pl
from jax.experimental.pallas import tpu as pltpu

N_HEADS = 32
QK_NOPE_DIM = 128
QK_ROPE_DIM = 64
V_DIM = 128
CONV_WIDTH = 31
LRU_CONV_WIDTH = 4
LRU_BLOCKS = 16
LRU_C = 8.0
ROPE_THETA = 10000.0
EPS = 1e-6

LANES = 128
SUBLANES = 8
CONV_HALO = 32
VMEM_LIMIT_BYTES = 52 * 1024 * 1024

F32 = jnp.float32
BF16 = jnp.bfloat16


def _cparams(*sem):
    return pltpu.CompilerParams(dimension_semantics=sem, vmem_limit_bytes=VMEM_LIMIT_BYTES)


def _tile(dim, pref):
    t = min(dim, pref)
    assert dim % t == 0, (dim, t)
    return t


def _rms(x, g):
    return x * lax.rsqrt(jnp.mean(x * x, axis=-1, keepdims=True) + EPS) * g


def _silu(x):
    return x * jax.nn.sigmoid(x)


def _rope_lanes(x, cos, sin_signed):
    lane = lax.broadcasted_iota(jnp.int32, x.shape, 1)
    first_half = (lane % QK_ROPE_DIM) < (QK_ROPE_DIM // 2)
    partner = jnp.where(first_half,
                        pltpu.roll(x, LANES - QK_ROPE_DIM // 2, 1),
                        pltpu.roll(x, QK_ROPE_DIM // 2, 1))
    return x * cos + partner * sin_signed


def _rope_table_kernel(pos_ref, freq_ref, sign_ref, cos_ref, sin_ref):
    ang = pos_ref[...].astype(F32) * freq_ref[...]
    cos_ref[...] = jnp.cos(ang)
    sin_ref[...] = jnp.sin(ang) * sign_ref[...]


def _rope_tables(positions):
    s = positions.shape[0]
    half = QK_ROPE_DIM // 2
    inv_freq = ROPE_THETA ** (-jnp.arange(0, QK_ROPE_DIM, 2, dtype=F32) / QK_ROPE_DIM)
    freq = jnp.tile(inv_freq, LANES // half)[None, :]
    sign = jnp.tile(jnp.concatenate([-jnp.ones((half,), F32), jnp.ones((half,), F32)]),
                    LANES // QK_ROPE_DIM)[None, :]
    t = _tile(s, 512)
    row = pl.BlockSpec((1, LANES), lambda i: (0, 0))
    out = pl.BlockSpec((t, LANES), lambda i: (i, 0))
    return pl.pallas_call(
        _rope_table_kernel,
        out_shape=(jax.ShapeDtypeStruct((s, LANES), F32),) * 2,
        grid=(s // t,),
        in_specs=[pl.BlockSpec((t, 1), lambda i: (i, 0)), row, row],
        out_specs=(out, out),
        compiler_params=_cparams("parallel"),
        name="rope_tables",
    )(positions.reshape(s, 1), freq, sign)


def _prenorm_kernel(x_ref, g_ref, h_ref):
    h_ref[...] = _rms(x_ref[...], g_ref[...]).astype(h_ref.dtype)


def _prenorm(x, g):
    s, d = x.shape
    t = _tile(s, 256)
    return pl.pallas_call(
        _prenorm_kernel,
        out_shape=jax.ShapeDtypeStruct((s, d), BF16),
        grid=(s // t,),
        in_specs=[pl.BlockSpec((t, d), lambda i: (i, 0)), pl.BlockSpec((1, d), lambda i: (0, 0))],
        out_specs=pl.BlockSpec((t, d), lambda i: (i, 0)),
        compiler_params=_cparams("parallel"),
        name="prenorm",
    )(x, g.reshape(1, d))


def _post_kernel(x_ref, o_ref, gpost_ref, gpre_ref, xn_ref, h_ref):
    xn = x_ref[...] + _rms(o_ref[...], gpost_ref[...])
    xn_ref[...] = xn
    h_ref[...] = _rms(xn, gpre_ref[...]).astype(h_ref.dtype)


def _post(x, out, g_post, g_pre_next):
    s, d = x.shape
    t = _tile(s, 256)
    tile = pl.BlockSpec((t, d), lambda i: (i, 0))
    row = pl.BlockSpec((1, d), lambda i: (0, 0))
    return pl.pallas_call(
        _post_kernel,
        out_shape=(jax.ShapeDtypeStruct((s, d), F32), jax.ShapeDtypeStruct((s, d), BF16)),
        grid=(s // t,),
        in_specs=[tile, tile, row, row],
        out_specs=(tile, tile),
        compiler_params=_cparams("parallel"),
        name="post",
    )(x, out, g_post.reshape(1, d), g_pre_next.reshape(1, d))


NT_DIMS = (((1,), (1,)), ((), ()))


def _dot_nt(a, w):
    return lax.dot_general(a, w, NT_DIMS, preferred_element_type=F32)


def _proj_kernel(a_ref, *refs, act, n_w, w_is_nk):
    w_refs, o_ref, w_scr = refs[:n_w], refs[n_w], refs[n_w + 1:]

    @pl.when(pl.program_id(1) == 0)
    def _():
        for w_ref, scr in zip(w_refs, w_scr):
            scr[...] = w_ref[0].astype(BF16)

    a = a_ref[...]
    dot = _dot_nt if w_is_nk else functools.partial(jnp.dot, preferred_element_type=F32)
    acc = dot(a, w_scr[0][...])
    if act == "glu":
        acc = acc * jax.nn.sigmoid(dot(a, w_scr[1][...]))
    elif act == "silu":
        acc = _silu(acc)
    elif act == "sigmoid":
        acc = jax.nn.sigmoid(acc)
    o_ref[...] = acc.astype(o_ref.dtype)


def _proj(a, w_all, layer, col0, ncols, out_dtype, act=None, w_is_nk=False, tm=1024, tn=512, name="proj"):
    m, k = a.shape
    tm, tn = _tile(m, tm), _tile(ncols, tn)
    n_w = 2 if act == "glu" else 1
    starts = [col0 + i * ncols for i in range(n_w)]
    if w_is_nk:
        assert all(c % SUBLANES == 0 for c in starts)
        w_specs = [pl.BlockSpec((pl.Element(1), pl.Element(tn), pl.Element(k)),
                                lambda j, r, c=c: (layer, pl.multiple_of(c + j * tn, SUBLANES), 0))
                   for c in starts]
        w_tile = (tn, k)
    else:
        assert all(c % tn == 0 for c in starts)
        w_specs = [pl.BlockSpec((1, k, tn), lambda j, r, c=c: (layer, 0, c // tn + j)) for c in starts]
        w_tile = (k, tn)
    return pl.pallas_call(
        functools.partial(_proj_kernel, act=act, n_w=n_w, w_is_nk=w_is_nk),
        out_shape=jax.ShapeDtypeStruct((m, ncols), out_dtype),
        grid=(ncols // tn, m // tm),
        in_specs=[pl.BlockSpec((tm, k), lambda j, r: (r, 0))] + w_specs,
        out_specs=pl.BlockSpec((tm, tn), lambda j, r: (r, j)),
        scratch_shapes=[pltpu.VMEM(w_tile, BF16)] * n_w,
        compiler_params=_cparams("parallel", "arbitrary"),
        name=name,
    )(a, *([w_all] * n_w))


def _cq_kernel(a_ref, w_ref, g_ref, o_ref):
    o_ref[...] = _rms(_dot_nt(a_ref[...], w_ref[...]), g_ref[...]).astype(o_ref.dtype)


def _cq_proj(a, w_nk, g, tm=512):
    m, k = a.shape
    n = w_nk.shape[0]
    tm = _tile(m, tm)
    return pl.pallas_call(
        _cq_kernel,
        out_shape=jax.ShapeDtypeStruct((m, n), BF16),
        grid=(m // tm,),
        in_specs=[pl.BlockSpec((tm, k), lambda i: (i, 0)),
                  pl.BlockSpec((n, k), lambda i: (0, 0)),
                  pl.BlockSpec((1, n), lambda i: (0, 0))],
        out_specs=pl.BlockSpec((tm, n), lambda i: (i, 0)),
        compiler_params=_cparams("parallel"),
        name="cq_proj",
    )(a, w_nk, g.reshape(1, n))


def _ckvr_kernel(a_ref, w_ref, g_ref, cos_ref, sin_ref, ckv_ref, kr_ref, *, kv_lora):
    acc = _dot_nt(a_ref[...], w_ref[...])
    ckv_ref[...] = _rms(acc[:, :kv_lora], g_ref[...]).astype(ckv_ref.dtype)
    kr_ref[...] = _rope_lanes(acc[:, kv_lora:], cos_ref[...], sin_ref[...]).astype(kr_ref.dtype)


def _ckvr_proj(a, w_nk, g, cos, sin, tm=512):
    m, k = a.shape
    n = w_nk.shape[0]
    kv_lora = n - LANES
    tm = _tile(m, tm)
    return pl.pallas_call(
        functools.partial(_ckvr_kernel, kv_lora=kv_lora),
        out_shape=(jax.ShapeDtypeStruct((m, kv_lora), BF16), jax.ShapeDtypeStruct((m, LANES), BF16)),
        grid=(m // tm,),
        in_specs=[pl.BlockSpec((tm, k), lambda i: (i, 0)),
                  pl.BlockSpec((n, k), lambda i: (0, 0)),
                  pl.BlockSpec((1, kv_lora), lambda i: (0, 0)),
                  pl.BlockSpec((tm, LANES), lambda i: (i, 0)),
                  pl.BlockSpec((tm, LANES), lambda i: (i, 0))],
        out_specs=(pl.BlockSpec((tm, kv_lora), lambda i: (i, 0)),
                   pl.BlockSpec((tm, LANES), lambda i: (i, 0))),
        compiler_params=_cparams("parallel"),
        name="ckvr_proj",
    )(a, w_nk, g.reshape(1, kv_lora), cos, sin)


def _conv_branch_kernel(cur_ref, prev_ref, gate_ref, w_ref, b_ref, lng_ref, lnb_ref, o_ref,
                        buf_ref, conv_ref, *, t):
    i = pl.program_id(0)
    c = cur_ref.shape[1]
    prev = prev_ref[...]
    buf_ref[0:CONV_HALO, :] = jnp.where(i > 0, prev, jnp.zeros_like(prev))
    buf_ref[CONV_HALO:, :] = cur_ref[...]
    aligned = CONV_HALO - SUBLANES
    for c0 in range(0, c, LANES):
        lanes = slice(c0, c0 + LANES)
        acc = jnp.zeros((t, LANES), F32) + b_ref[:, lanes]
        for r in range(SUBLANES):
            z = buf_ref[pl.ds(SUBLANES - r, t + aligned), lanes]
            part = None
            for q in range(CONV_HALO // SUBLANES):
                d = SUBLANES * q + r
                if d >= CONV_WIDTH:
                    continue
                off = aligned - SUBLANES * q
                term = w_ref[pl.ds(CONV_WIDTH - 1 - d, 1), lanes] * z[off:off + t, :]
                part = term if part is None else part + term
            acc = acc + part
        conv_ref[:, lanes] = acc
    y = conv_ref[...]
    mu = jnp.mean(y, axis=-1, keepdims=True)
    yc = y - mu
    var = jnp.mean(yc * yc, axis=-1, keepdims=True)
    y = yc * lax.rsqrt(var + EPS) * lng_ref[...] + lnb_ref[...]
    o_ref[...] = (_silu(y) * gate_ref[...].astype(F32)).astype(o_ref.dtype)


def _conv_branch(y_glu, gate, dw_w, dw_b, ln_g, ln_b, t=128):
    s, c = y_glu.shape
    t = _tile(s, t)
    assert t % CONV_HALO == 0
    w = jnp.pad(dw_w, ((0, CONV_HALO - CONV_WIDTH), (0, 0)))
    row = pl.BlockSpec((1, c), lambda i: (0, 0))
    return pl.pallas_call(
        functools.partial(_conv_branch_kernel, t=t),
        out_shape=jax.ShapeDtypeStruct((s, c), BF16),
        grid=(s // t,),
        in_specs=[pl.BlockSpec((t, c), lambda i: (i, 0)),
                  pl.BlockSpec((CONV_HALO, c), lambda i: (jnp.maximum(i * (t // CONV_HALO) - 1, 0), 0)),
                  pl.BlockSpec((t, c), lambda i: (i, 0)),
                  pl.BlockSpec((CONV_HALO, c), lambda i: (0, 0)),
                  row, row, row],
        out_specs=pl.BlockSpec((t, c), lambda i: (i, 0)),
        scratch_shapes=[pltpu.VMEM((t + CONV_HALO, c), F32), pltpu.VMEM((t, c), F32)],
        compiler_params=_cparams("parallel"),
        name="conv_branch",
    )(y_glu, y_glu, gate, w, dw_b.reshape(1, c), ln_g.reshape(1, c), ln_b.reshape(1, c))


def _lru_kernel(cur_ref, prev_ref, gate_ref, cw_ref, cb_ref, wa_ref, ba_ref, wx_ref, bx_ref,
                sp_ref, o_ref, buf_ref, a_scr, b_scr, h_scr, carry_ref, *, t):
    i = pl.program_id(0)
    c = cur_ref.shape[1]
    prev = prev_ref[...]
    buf_ref[0:SUBLANES, :] = jnp.where(i > 0, prev, jnp.zeros_like(prev))
    buf_ref[SUBLANES:, :] = cur_ref[...]

    @pl.when(i == 0)
    def _():
        carry_ref[...] = jnp.zeros_like(carry_ref)

    xc = jnp.zeros((t, c), F32) + cb_ref[...]
    for k in range(LRU_CONV_WIDTH):
        off = SUBLANES - (LRU_CONV_WIDTH - 1) + k
        xc = xc + cw_ref[pl.ds(k, 1), :] * buf_ref[pl.ds(off, t), :]

    xb = xc.astype(BF16)
    nblk = wa_ref.shape[0]
    bd = c // nblk
    ra = jnp.concatenate(
        [jnp.dot(xb[:, g * bd:(g + 1) * bd], wa_ref[g], preferred_element_type=F32) for g in range(nblk)],
        axis=1)
    rx = jnp.concatenate(
        [jnp.dot(xb[:, g * bd:(g + 1) * bd], wx_ref[g], preferred_element_type=F32) for g in range(nblk)],
        axis=1)
    r = jax.nn.sigmoid(ra + ba_ref[...])
    gi = jax.nn.sigmoid(rx + bx_ref[...])
    a = jnp.exp((-LRU_C) * r * sp_ref[...])
    v = 1.0 - a * a
    b = jnp.where(v > 0.0, v * lax.rsqrt(v), 0.0) * (gi * xc)

    a3 = a.reshape(t // SUBLANES, SUBLANES, c)
    b3 = b.reshape(t // SUBLANES, SUBLANES, c)
    sub = lax.broadcasted_iota(jnp.int32, a3.shape, 1)
    for d in (1, 2, 4):
        keep = sub >= d
        a_sh = pltpu.roll(a3, d, 1)
        b_sh = pltpu.roll(b3, d, 1)
        b3 = jnp.where(keep, a3 * b_sh + b3, b3)
        a3 = jnp.where(keep, a3 * a_sh, a3)
    a_scr[...] = a3
    b_scr[...] = b3

    h_last = carry_ref[...]
    for j in range(t // SUBLANES):
        hj = a_scr[j] * h_last + b_scr[j]
        h_scr[j] = hj
        h_last = hj[SUBLANES - 1:SUBLANES, :]
    carry_ref[...] = h_last

    h = h_scr[...].reshape(t, c)
    o_ref[...] = (h * gate_ref[...].astype(F32)).astype(o_ref.dtype)


def _lru_branch(u, gate, cw, cb, w_a, b_a, w_x, b_x, lam, t=256):
    s, c = u.shape
    t = _tile(s, t)
    nblk, bd, _ = w_a.shape
    row = pl.BlockSpec((1, c), lambda i: (0, 0))
    wblk = pl.BlockSpec((nblk, bd, bd), lambda i: (0, 0, 0))
    cw_pad = jnp.pad(cw, ((0, SUBLANES - LRU_CONV_WIDTH), (0, 0)))
    return pl.pallas_call(
        functools.partial(_lru_kernel, t=t),
        out_shape=jax.ShapeDtypeStruct((s, c), BF16),
        grid=(s // t,),
        in_specs=[pl.BlockSpec((t, c), lambda i: (i, 0)),
                  pl.BlockSpec((SUBLANES, c), lambda i: (jnp.maximum(i * (t // SUBLANES) - 1, 0), 0)),
                  pl.BlockSpec((t, c), lambda i: (i, 0)),
                  pl.BlockSpec((SUBLANES, c), lambda i: (0, 0)),
                  row, wblk, row, wblk, row, row],
        out_specs=pl.BlockSpec((t, c), lambda i: (i, 0)),
        scratch_shapes=[pltpu.VMEM((t + SUBLANES, c), F32),
                        pltpu.VMEM((t // SUBLANES, SUBLANES, c), F32),
                        pltpu.VMEM((t // SUBLANES, SUBLANES, c), F32),
                        pltpu.VMEM((t // SUBLANES, SUBLANES, c), F32),
                        pltpu.VMEM((1, c), F32)],
        compiler_params=_cparams("arbitrary"),
        name="lru_branch",
    )(u, u, gate, cw_pad, cb.reshape(1, c), w_a, b_a.reshape(1, c), w_x, b_x.reshape(1, c),
      jax.nn.softplus(-lam).reshape(1, c))


HEAD_GROUP = 8
QK_PAD = 2 * LANES


def _q_kernel(a_ref, wn_ref, wr_ref, cos_ref, sin_ref, o_ref, *, scale):
    a = a_ref[...]
    qn = jnp.dot(a, wn_ref[...], preferred_element_type=F32) * scale
    qr = jnp.dot(a, wr_ref[...], preferred_element_type=F32)
    cos, sin = cos_ref[...], sin_ref[...]
    lane = lax.broadcasted_iota(jnp.int32, cos.shape, 1)
    for pair in range(HEAD_GROUP // 2):
        roped = _rope_lanes(qr[:, pair * LANES:(pair + 1) * LANES], cos, sin) * scale
        halves = (jnp.where(lane < QK_ROPE_DIM, roped, 0.0), jnp.where(lane >= QK_ROPE_DIM, roped, 0.0))
        for sub in range(2):
            h = 2 * pair + sub
            o_ref[h, :, 0:QK_NOPE_DIM] = qn[:, h * QK_NOPE_DIM:(h + 1) * QK_NOPE_DIM].astype(o_ref.dtype)
            o_ref[h, :, QK_NOPE_DIM:QK_PAD] = halves[sub].astype(o_ref.dtype)


def _q_proj(cqn, w_qn, w_qr, cos, sin, scale, tm=1024):
    m, k = cqn.shape
    nh = w_qn.shape[1] // QK_NOPE_DIM
    assert nh % HEAD_GROUP == 0
    tm = _tile(m, tm)
    return pl.pallas_call(
        functools.partial(_q_kernel, scale=scale),
        out_shape=jax.ShapeDtypeStruct((nh, m, QK_PAD), BF16),
        grid=(m // tm, nh // HEAD_GROUP),
        in_specs=[pl.BlockSpec((tm, k), lambda i, g: (i, 0)),
                  pl.BlockSpec((k, HEAD_GROUP * QK_NOPE_DIM), lambda i, g: (0, g)),
                  pl.BlockSpec((k, HEAD_GROUP * QK_ROPE_DIM), lambda i, g: (0, g)),
                  pl.BlockSpec((tm, LANES), lambda i, g: (i, 0)),
                  pl.BlockSpec((tm, LANES), lambda i, g: (i, 0))],
        out_specs=pl.BlockSpec((HEAD_GROUP, tm, QK_PAD), lambda i, g: (g, i, 0)),
        compiler_params=_cparams("parallel", "parallel"),
        name="q_proj",
    )(cqn, w_qn, w_qr, cos, sin)


def _kv_kernel(a_ref, wk_ref, wv_ref, kr_ref, k_ref, v_ref):
    a = a_ref[...]
    kn = jnp.dot(a, wk_ref[...], preferred_element_type=F32)
    v = jnp.dot(a, wv_ref[...], preferred_element_type=F32)
    kr = kr_ref[...]
    for h in range(HEAD_GROUP):
        k_ref[h, :, 0:QK_NOPE_DIM] = kn[:, h * QK_NOPE_DIM:(h + 1) * QK_NOPE_DIM].astype(k_ref.dtype)
        k_ref[h, :, QK_NOPE_DIM:QK_PAD] = kr
        v_ref[h] = v[:, h * V_DIM:(h + 1) * V_DIM].astype(v_ref.dtype)


def _kv_proj(ckvn, w_kn, w_v, kr, tm=1024):
    m, k = ckvn.shape
    nh = w_kn.shape[1] // QK_NOPE_DIM
    tm = _tile(m, tm)
    return pl.pallas_call(
        _kv_kernel,
        out_shape=(jax.ShapeDtypeStruct((nh, m, QK_PAD), BF16), jax.ShapeDtypeStruct((nh, m, V_DIM), BF16)),
        grid=(m // tm, nh // HEAD_GROUP),
        in_specs=[pl.BlockSpec((tm, k), lambda i, g: (i, 0)),
                  pl.BlockSpec((k, HEAD_GROUP * QK_NOPE_DIM), lambda i, g: (0, g)),
                  pl.BlockSpec((k, HEAD_GROUP * V_DIM), lambda i, g: (0, g)),
                  pl.BlockSpec((tm, LANES), lambda i, g: (i, 0))],
        out_specs=(pl.BlockSpec((HEAD_GROUP, tm, QK_PAD), lambda i, g: (g, i, 0)),
                   pl.BlockSpec((HEAD_GROUP, tm, V_DIM), lambda i, g: (g, i, 0))),
        compiler_params=_cparams("parallel", "parallel"),
        name="kv_proj",
    )(ckvn, w_kn, w_v, kr)


FLASH_HEADS = 2
FLASH_TQ = 1024
FLASH_TK = 1024
FLASH_TD = 512


def _flash_kernel(q_ref, k_ref, v_ref, g_ref, o_ref, m_scr, l_scr, acc_scr, *, tq, tk, td):
    qi = pl.program_id(1)
    nheads = q_ref.shape[0]
    m_scr[...] = jnp.full_like(m_scr, -jnp.inf)
    l_scr[...] = jnp.zeros_like(l_scr)
    acc_scr[...] = jnp.zeros_like(acc_scr)

    def step(off, klen, r0, diagonal):
        rows = slice(r0, tq)
        for h in range(nheads):
            q = q_ref[h, rows, :]
            k = k_ref[h, pl.ds(off, klen), :]
            v = v_ref[h, pl.ds(off, klen), :]
            s = lax.dot_general(q, k, (((1,), (1,)), ((), ())), preferred_element_type=F32)
            if diagonal:
                row = lax.broadcasted_iota(jnp.int32, s.shape, 0)
                col = lax.broadcasted_iota(jnp.int32, s.shape, 1)
                s = jnp.where(col <= row, s, jnp.finfo(F32).min)
            m_prev = m_scr[h, rows, :]
            m_new = jnp.maximum(m_prev, jnp.max(s, axis=1, keepdims=True))
            alpha = jnp.exp2(m_prev - m_new)
            p = jnp.exp2(s - jnp.tile(m_new, (1, klen // LANES)))
            psum = p[:, 0:LANES]
            for c in range(1, klen // LANES):
                psum = psum + p[:, c * LANES:(c + 1) * LANES]
            l_scr[h, rows, :] = alpha * l_scr[h, rows, :] + psum
            acc_scr[h, rows, :] = alpha * acc_scr[h, rows, :] + jnp.dot(
                p.astype(v.dtype), v, preferred_element_type=F32)
            m_scr[h, rows, :] = m_new

    def body(ki, carry):
        step(pl.multiple_of(ki * tk, tk), tk, 0, False)
        return carry

    n_full = qi * (tq // tk)

    def body2(ki, carry):
        step(pl.multiple_of(2 * ki * tk, tk), tk, 0, False)
        step(pl.multiple_of((2 * ki + 1) * tk, tk), tk, 0, False)
        return carry

    lax.fori_loop(0, n_full // 2, body2, 0)
    lax.fori_loop(n_full - n_full % 2, n_full, body, 0)
    q0 = pl.multiple_of(qi * tq, tq)
    for r in range(tq // td):
        step(q0 + r * td, td, r * td, True)
    for h in range(nheads):
        cols = slice(h * V_DIM, (h + 1) * V_DIM)
        o = acc_scr[h] / jnp.sum(l_scr[h], axis=1, keepdims=True)
        o_ref[:, cols] = (o * g_ref[:, cols].astype(F32)).astype(o_ref.dtype)


def _flash(q, k, v, gate):
    nh, s, _ = q.shape
    hp = FLASH_HEADS
    tq = _tile(s, FLASH_TQ)
    tk = _tile(tq, FLASH_TK)
    td = _tile(tk, FLASH_TD)
    assert nh % hp == 0
    return pl.pallas_call(
        functools.partial(_flash_kernel, tq=tq, tk=tk, td=td),
        out_shape=jax.ShapeDtypeStruct((s, nh * V_DIM), BF16),
        grid=(nh // hp, s // tq),
        in_specs=[pl.BlockSpec((hp, tq, QK_PAD), lambda h, i: (h, i, 0)),
                  pl.BlockSpec((hp, s, QK_PAD), lambda h, i: (h, 0, 0)),
                  pl.BlockSpec((hp, s, V_DIM), lambda h, i: (h, 0, 0)),
                  pl.BlockSpec((tq, hp * V_DIM), lambda h, i: (i, h))],
        out_specs=pl.BlockSpec((tq, hp * V_DIM), lambda h, i: (i, h)),
        scratch_shapes=[pltpu.VMEM((hp, tq, LANES), F32), pltpu.VMEM((hp, tq, LANES), F32),
                        pltpu.VMEM((hp, tq, V_DIM), F32)],
        compiler_params=_cparams("parallel", "parallel"),
        name="flash",
    )(q, k, v, gate)


def _merge_kernel(ya_ref, yb_ref, yc_ref, wa_ref, wb_ref, wc_ref, g0_ref, g1_ref, g2_ref, o_ref):
    pa = jnp.dot(ya_ref[...], wa_ref[...], preferred_element_type=F32)
    pb = jnp.dot(yb_ref[...], wb_ref[...], preferred_element_type=F32)
    pc = jnp.dot(yc_ref[...], wc_ref[...], preferred_element_type=F32)
    merged = (g0_ref[...].astype(F32) * pa + g1_ref[...].astype(F32) * pb) + g2_ref[...].astype(F32) * pc
    o_ref[...] = merged.astype(o_ref.dtype)


def _merge(ya, yb, yc, wa, wb, wc, gm, tm=512, tn=512):
    m = ya.shape[0]
    n = wa.shape[1]
    tm, tn = _tile(m, tm), _tile(n, tn)
    nb = n // tn
    lhs = lambda y: pl.BlockSpec((tm, y.shape[1]), lambda i, j: (i, 0))
    rhs = lambda w: pl.BlockSpec((w.shape[0], tn), lambda i, j: (0, j))
    gate = lambda b: pl.BlockSpec((tm, tn), lambda i, j: (i, j + b * nb))
    return pl.pallas_call(
        _merge_kernel,
        out_shape=jax.ShapeDtypeStruct((m, n), BF16),
        grid=(m // tm, nb),
        in_specs=[lhs(ya), lhs(yb), lhs(yc), rhs(wa), rhs(wb), rhs(wc), gate(0), gate(1), gate(2)],
        out_specs=pl.BlockSpec((tm, tn), lambda i, j: (i, j)),
        compiler_params=_cparams("parallel", "parallel"),
        name="merge",
    )(ya, yb, yc, wa, wb, wc, gm, gm, gm)


def kernel(x, positions, pre_norm_g, w_in, conv_dw_w, conv_dw_b, conv_ln_g, conv_ln_b, w_conv_proj, q_norm_g, w_uq, kv_norm_g, w_ukv, w_mla_proj, lru_conv_w, lru_conv_b, lru_w_a, lru_b_a, lru_w_x, lru_b_x, lru_lambda, w_lru_proj, w_out, post_norm_g):
    batch, seq, d_model = x.shape
    assert batch == 1
    depth = w_in.shape[0]
    nh = N_HEADS
    d_conv = conv_dw_w.shape[2]
    d_lru = lru_conv_w.shape[2]
    q_lora = w_uq.shape[1]
    kv_lora = w_ukv.shape[1]
    d_mla = nh * V_DIM
    qk_dim = QK_NOPE_DIM + QK_ROPE_DIM
    scale = qk_dim ** -0.5 * float(np.log2(np.e))
    splits = np.cumsum([0, 2 * d_conv, d_conv, q_lora, kv_lora, QK_ROPE_DIM, d_mla, d_lru, d_lru, 3 * d_model])
    (c_glu, c_gconv, c_cq, c_ckv, c_kr, c_gmla, c_ulru, c_glru, c_gm, c_end) = [int(v) for v in splits]
    assert c_end == w_in.shape[2]

    cos, sin = _rope_tables(positions[0])
    xs = x[0]
    h = _prenorm(xs, pre_norm_g[0])

    w_in_nk = jnp.swapaxes(w_in, 1, 2)
    inproj = functools.partial(_proj, w_is_nk=True)

    for l in range(depth):
        cast = lambda w: w.astype(BF16)
        w_lat = lax.optimization_barrier(w_in_nk[l, c_cq:c_gmla])
        w_cq = cast(w_lat[:q_lora])
        w_ckvr = cast(jnp.concatenate([w_lat[q_lora:], w_lat[q_lora + kv_lora:]], axis=0))
        uq = w_uq[l].reshape(q_lora, nh, qk_dim)
        w_qn = cast(uq[:, :, :QK_NOPE_DIM].reshape(q_lora, nh * QK_NOPE_DIM))
        w_qr = cast(uq[:, :, QK_NOPE_DIM:].reshape(q_lora, nh * QK_ROPE_DIM))
        ukv = w_ukv[l].reshape(kv_lora, nh, QK_NOPE_DIM + V_DIM)
        w_kn = cast(ukv[:, :, :QK_NOPE_DIM].reshape(kv_lora, nh * QK_NOPE_DIM))
        w_v = cast(ukv[:, :, QK_NOPE_DIM:].reshape(kv_lora, nh * V_DIM))

        y_glu = inproj(h, w_in_nk, l, c_glu, d_conv, F32, act="glu", tn=256, name="glu_proj")
        cqn = _cq_proj(h, w_cq, q_norm_g[l])
        ckvn, kr = _ckvr_proj(h, w_ckvr, kv_norm_g[l], cos, sin)
        u_lru = inproj(h, w_in_nk, l, c_ulru, d_lru, F32, name="ulru_proj")
        g_conv = inproj(h, w_in_nk, l, c_gconv, d_conv, BF16, act="silu", name="conv_gate_proj")
        g_mla = inproj(h, w_in_nk, l, c_gmla, d_mla, BF16, act="silu", name="mla_gate_proj")
        g_lru = inproj(h, w_in_nk, l, c_glru, d_lru, BF16, act="silu", name="lru_gate_proj")
        gm = inproj(h, w_in_nk, l, c_gm, 3 * d_model, BF16, act="sigmoid", name="merge_gate_proj")

        y_a = _conv_branch(y_glu, g_conv, conv_dw_w[l], conv_dw_b[l], conv_ln_g[l], conv_ln_b[l])
        q = _q_proj(cqn, w_qn, w_qr, cos, sin, scale)
        k, v = _kv_proj(ckvn, w_kn, w_v, kr)
        y_b = _flash(q, k, v, g_mla)
        y_c = _lru_branch(u_lru, g_lru, lru_conv_w[l], lru_conv_b[l], cast(lru_w_a[l]),
                          lru_b_a[l], cast(lru_w_x[l]), lru_b_x[l], lru_lambda[l])

        merged = _merge(y_a, y_b, y_c, cast(w_conv_proj[l]), cast(w_mla_proj[l]), cast(w_lru_proj[l]), gm)
        out = _proj(merged, w_out, l, 0, d_model, F32, name="out_proj")
        g_next = pre_norm_g[l + 1] if l + 1 < depth else pre_norm_g[l]
        xs, h = _post(xs, out, post_norm_g[l], g_next)
    return xs[None]
```

```python
import functools

import jax
import jax.numpy as jnp
import numpy as np
from jax import lax
from jax.experimental import pallas as pl
from jax.experimental.pallas import tpu as pltpu

N_HEADS = 32
QK_NOPE_DIM = 128
QK_ROPE_DIM = 64
V_DIM = 128
CONV_WIDTH = 31
LRU_CONV_WIDTH = 4
LRU_BLOCKS = 16
LRU_C = 8.0
ROPE_THETA = 10000.0
EPS = 1e-6

LANES = 128
SUBLANES = 8
CONV_HALO = 32
VMEM_LIMIT_BYTES = 52 * 1024 * 1024

F32 = jnp.float32
BF16 = jnp.bfloat16


def _cparams(*sem):
    return pltpu.CompilerParams(dimension_semantics=sem, vmem_limit_bytes=VMEM_LIMIT_BYTES)


def _tile(dim, pref):
    t = min(dim, pref)
    assert dim % t == 0, (dim, t)
    return t


def _rms(x, g):
    return x * lax.rsqrt(jnp.mean(x * x, axis=-1, keepdims=True) + EPS) * g


def _silu(x):
    return x * jax.nn.sigmoid(x)


def _rope_lanes(x, cos, sin_signed):
    lane = lax.broadcasted_iota(jnp.int32, x.shape, 1)
    first_half = (lane % QK_ROPE_DIM) < (QK_ROPE_DIM // 2)
    partner = jnp.where(first_half,
                        pltpu.roll(x, LANES - QK_ROPE_DIM // 2, 1),
                        pltpu.roll(x, QK_ROPE_DIM // 2, 1))
    return x * cos + partner * sin_signed


def _rope_table_kernel(pos_ref, freq_ref, sign_ref, cos_ref, sin_ref):
    ang = pos_ref[...].astype(F32) * freq_ref[...]
    cos_ref[...] = jnp.cos(ang)
    sin_ref[...] = jnp.sin(ang) * sign_ref[...]


def _rope_tables(positions):
    s = positions.shape[0]
    half = QK_ROPE_DIM // 2
    inv_freq = ROPE_THETA ** (-jnp.arange(0, QK_ROPE_DIM, 2, dtype=F32) / QK_ROPE_DIM)
    freq = jnp.tile(inv_freq, LANES // half)[None, :]
    sign = jnp.tile(jnp.concatenate([-jnp.ones((half,), F32), jnp.ones((half,), F32)]),
                    LANES // QK_ROPE_DIM)[None, :]
    t = _tile(s, 512)
    row = pl.BlockSpec((1, LANES), lambda i: (0, 0))
    out = pl.BlockSpec((t, LANES), lambda i: (i, 0))
    return pl.pallas_call(
        _rope_table_kernel,
        out_shape=(jax.ShapeDtypeStruct((s, LANES), F32),) * 2,
        grid=(s // t,),
        in_specs=[pl.BlockSpec((t, 1), lambda i: (i, 0)), row, row],
        out_specs=(out, out),
        compiler_params=_cparams("parallel"),
        name="rope_tables",
    )(positions.reshape(s, 1), freq, sign)


def _prenorm_kernel(x_ref, g_ref, h_ref):
    h_ref[...] = _rms(x_ref[...], g_ref[...]).astype(h_ref.dtype)


def _prenorm(x, g):
    s, d = x.shape
    t = _tile(s, 256)
    return pl.pallas_call(
        _prenorm_kernel,
        out_shape=jax.ShapeDtypeStruct((s, d), BF16),
        grid=(s // t,),
        in_specs=[pl.BlockSpec((t, d), lambda i: (i, 0)), pl.BlockSpec((1, d), lambda i: (0, 0))],
        out_specs=pl.BlockSpec((t, d), lambda i: (i, 0)),
        compiler_params=_cparams("parallel"),
        name="prenorm",
    )(x, g.reshape(1, d))


def _post_kernel(x_ref, o_ref, gpost_ref, gpre_ref, xn_ref, h_ref):
    xn = x_ref[...] + _rms(o_ref[...], gpost_ref[...])
    xn_ref[...] = xn
    h_ref[...] = _rms(xn, gpre_ref[...]).astype(h_ref.dtype)


def _post(x, out, g_post, g_pre_next):
    s, d = x.shape
    t = _tile(s, 256)
    tile = pl.BlockSpec((t, d), lambda i: (i, 0))
    row = pl.BlockSpec((1, d), lambda i: (0, 0))
    return pl.pallas_call(
        _post_kernel,
        out_shape=(jax.ShapeDtypeStruct((s, d), F32), jax.ShapeDtypeStruct((s, d), BF16)),
        grid=(s // t,),
        in_specs=[tile, tile, row, row],
        out_specs=(tile, tile),
        compiler_params=_cparams("parallel"),
        name="post",
    )(x, out, g_post.reshape(1, d), g_pre_next.reshape(1, d))


NT_DIMS = (((1,), (1,)), ((), ()))


def _dot_nt(a, w):
    return lax.dot_general(a, w, NT_DIMS, preferred_element_type=F32)


def _proj_kernel(a_ref, *refs, act, n_w, w_is_nk):
    w_refs, o_ref, w_scr = refs[:n_w], refs[n_w], refs[n_w + 1:]

    @pl.when(pl.program_id(1) == 0)
    def _():
        for w_ref, scr in zip(w_refs, w_scr):
            scr[...] = w_ref[0].astype(BF16)

    a = a_ref[...]
    dot = _dot_nt if w_is_nk else functools.partial(jnp.dot, preferred_element_type=F32)
    acc = dot(a, w_scr[0][...])
    if act == "glu":
        acc = acc * jax.nn.sigmoid(dot(a, w_scr[1][...]))
    elif act == "silu":
        acc = _silu(acc)
    elif act == "sigmoid":
        acc = jax.nn.sigmoid(acc)
    o_ref[...] = acc.astype(o_ref.dtype)


def _proj(a, w_all, layer, col0, ncols, out_dtype, act=None, w_is_nk=False, tm=1024, tn=512, name="proj"):
    m, k = a.shape
    tm, tn = _tile(m, tm), _tile(ncols, tn)
    n_w = 2 if act == "glu" else 1
    starts = [col0 + i * ncols for i in range(n_w)]
    if w_is_nk:
        assert all(c % SUBLANES == 0 for c in starts)
        w_specs = [pl.BlockSpec((pl.Element(1), pl.Element(tn), pl.Element(k)),
                                lambda j, r, c=c: (layer, pl.multiple_of(c + j * tn, SUBLANES), 0))
                   for c in starts]
        w_tile = (tn, k)
    else:
        assert all(c % tn == 0 for c in starts)
        w_specs = [pl.BlockSpec((1, k, tn), lambda j, r, c=c: (layer, 0, c // tn + j)) for c in starts]
        w_tile = (k, tn)
    return pl.pallas_call(
        functools.partial(_proj_kernel, act=act, n_w=n_w, w_is_nk=w_is_nk),
        out_shape=jax.ShapeDtypeStruct((m, ncols), out_dtype),
        grid=(ncols // tn, m // tm),
        in_specs=[pl.BlockSpec((tm, k), lambda j, r: (r, 0))] + w_specs,
        out_specs=pl.BlockSpec((tm, tn), lambda j, r: (r, j)),
        scratch_shapes=[pltpu.VMEM(w_tile, BF16)] * n_w,
        compiler_params=_cparams("parallel", "arbitrary"),
        name=name,
    )(a, *([w_all] * n_w))


def _cq_kernel(a_ref, w_ref, g_ref, o_ref):
    o_ref[...] = _rms(_dot_nt(a_ref[...], w_ref[...]), g_ref[...]).astype(o_ref.dtype)


def _cq_proj(a, w_nk, g, tm=512):
    m, k = a.shape
    n = w_nk.shape[0]
    tm = _tile(m, tm)
    return pl.pallas_call(
        _cq_kernel,
        out_shape=jax.ShapeDtypeStruct((m, n), BF16),
        grid=(m // tm,),
        in_specs=[pl.BlockSpec((tm, k), lambda i: (i, 0)),
                  pl.BlockSpec((n, k), lambda i: (0, 0)),
                  pl.BlockSpec((1, n), lambda i: (0, 0))],
        out_specs=pl.BlockSpec((tm, n), lambda i: (i, 0)),
        compiler_params=_cparams("parallel"),
        name="cq_proj",
    )(a, w_nk, g.reshape(1, n))


def _ckvr_kernel(a_ref, w_ref, g_ref, cos_ref, sin_ref, ckv_ref, kr_ref, *, kv_lora):
    acc = _dot_nt(a_ref[...], w_ref[...])
    ckv_ref[...] = _rms(acc[:, :kv_lora], g_ref[...]).astype(ckv_ref.dtype)
    kr_ref[...] = _rope_lanes(acc[:, kv_lora:], cos_ref[...], sin_ref[...]).astype(kr_ref.dtype)


def _ckvr_proj(a, w_nk, g, cos, sin, tm=512):
    m, k = a.shape
    n = w_nk.shape[0]
    kv_lora = n - LANES
    tm = _tile(m, tm)
    return pl.pallas_call(
        functools.partial(_ckvr_kernel, kv_lora=kv_lora),
        out_shape=(jax.ShapeDtypeStruct((m, kv_lora), BF16), jax.ShapeDtypeStruct((m, LANES), BF16)),
        grid=(m // tm,),
        in_specs=[pl.BlockSpec((tm, k), lambda i: (i, 0)),
                  pl.BlockSpec((n, k), lambda i: (0, 0)),
                  pl.BlockSpec((1, kv_lora), lambda i: (0, 0)),
                  pl.BlockSpec((tm, LANES), lambda i: (i, 0)),
                  pl.BlockSpec((tm, LANES), lambda i: (i, 0))],
        out_specs=(pl.BlockSpec((tm, kv_lora), lambda i: (i, 0)),
                   pl.BlockSpec((tm, LANES), lambda i: (i, 0))),
        compiler_params=_cparams("parallel"),
        name="ckvr_proj",
    )(a, w_nk, g.reshape(1, kv_lora), cos, sin)


def _conv_branch_kernel(cur_ref, prev_ref, gate_ref, w_ref, b_ref, lng_ref, lnb_ref, o_ref,
                        buf_ref, conv_ref, *, t):
    i = pl.program_id(0)
    c = cur_ref.shape[1]
    prev = prev_ref[...]
    buf_ref[0:CONV_HALO, :] = jnp.where(i > 0, prev, jnp.zeros_like(prev))
    buf_ref[CONV_HALO:, :] = cur_ref[...]
    aligned = CONV_HALO - SUBLANES
    for c0 in range(0, c, LANES):
        lanes = slice(c0, c0 + LANES)
        acc = jnp.zeros((t, LANES), F32) + b_ref[:, lanes]
        for r in range(SUBLANES):
            z = buf_ref[pl.ds(SUBLANES - r, t + aligned), lanes]
            part = None
            for q in range(CONV_HALO // SUBLANES):
                d = SUBLANES * q + r
                if d >= CONV_WIDTH:
                    continue
                off = aligned - SUBLANES * q
                term = w_ref[pl.ds(CONV_WIDTH - 1 - d, 1), lanes] * z[off:off + t, :]
                part = term if part is None else part + term
            acc = acc + part
        conv_ref[:, lanes] = acc
    y = conv_ref[...]
    mu = jnp.mean(y, axis=-1, keepdims=True)
    yc = y - mu
    var = jnp.mean(yc * yc, axis=-1, keepdims=True)
    y = yc * lax.rsqrt(var + EPS) * lng_ref[...] + lnb_ref[...]
    o_ref[...] = (_silu(y) * gate_ref[...].astype(F32)).astype(o_ref.dtype)


def _conv_branch(y_glu, gate, dw_w, dw_b, ln_g, ln_b, t=128):
    s, c = y_glu.shape
    t = _tile(s, t)
    assert t % CONV_HALO == 0
    w = jnp.pad(dw_w, ((0, CONV_HALO - CONV_WIDTH), (0, 0)))
    row = pl.BlockSpec((1, c), lambda i: (0, 0))
    return pl.pallas_call(
        functools.partial(_conv_branch_kernel, t=t),
        out_shape=jax.ShapeDtypeStruct((s, c), BF16),
        grid=(s // t,),
        in_specs=[pl.BlockSpec((t, c), lambda i: (i, 0)),
                  pl.BlockSpec((CONV_HALO, c), lambda i: (jnp.maximum(i * (t // CONV_HALO) - 1, 0), 0)),
                  pl.BlockSpec((t, c), lambda i: (i, 0)),
                  pl.BlockSpec((CONV_HALO, c), lambda i: (0, 0)),
                  row, row, row],
        out_specs=pl.BlockSpec((t, c), lambda i: (i, 0)),
        scratch_shapes=[pltpu.VMEM((t + CONV_HALO, c), F32), pltpu.VMEM((t, c), F32)],
        compiler_params=_cparams("parallel"),
        name="conv_branch",
    )(y_glu, y_glu, gate, w, dw_b.reshape(1, c), ln_g.reshape(1, c), ln_b.reshape(1, c))


def _lru_kernel(cur_ref, prev_ref, gate_ref, cw_ref, cb_ref, wa_ref, ba_ref, wx_ref, bx_ref,
                sp_ref, o_ref, buf_ref, a_scr, b_scr, h_scr, carry_ref, *, t):
    i = pl.program_id(0)
    c = cur_ref.shape[1]
    prev = prev_ref[...]
    buf_ref[0:SUBLANES, :] = jnp.where(i > 0, prev, jnp.zeros_like(prev))
    buf_ref[SUBLANES:, :] = cur_ref[...]

    @pl.when(i == 0)
    def _():
        carry_ref[...] = jnp.zeros_like(carry_ref)

    xc = jnp.zeros((t, c), F32) + cb_ref[...]
    for k in range(LRU_CONV_WIDTH):
        off = SUBLANES - (LRU_CONV_WIDTH - 1) + k
        xc = xc + cw_ref[pl.ds(k, 1), :] * buf_ref[pl.ds(off, t), :]

    xb = xc.astype(BF16)
    nblk = wa_ref.shape[0]
    bd = c // nblk
    ra = jnp.concatenate(
        [jnp.dot(xb[:, g * bd:(g + 1) * bd], wa_ref[g], preferred_element_type=F32) for g in range(nblk)],
        axis=1)
    rx = jnp.concatenate(
        [jnp.dot(xb[:, g * bd:(g + 1) * bd], wx_ref[g], preferred_element_type=F32) for g in range(nblk)],
        axis=1)
    r = jax.nn.sigmoid(ra + ba_ref[...])
    gi = jax.nn.sigmoid(rx + bx_ref[...])
    a = jnp.exp((-LRU_C) * r * sp_ref[...])
    v = 1.0 - a * a
    b = jnp.where(v > 0.0, v * lax.rsqrt(v), 0.0) * (gi * xc)

    a3 = a.reshape(t // SUBLANES, SUBLANES, c)
    b3 = b.reshape(t // SUBLANES, SUBLANES, c)
    sub = lax.broadcasted_iota(jnp.int32, a3.shape, 1)
    for d in (1, 2, 4):
        keep = sub >= d
        a_sh = pltpu.roll(a3, d, 1)
        b_sh = pltpu.roll(b3, d, 1)
        b3 = jnp.where(keep, a3 * b_sh + b3, b3)
        a3 = jnp.where(keep, a3 * a_sh, a3)
    a_scr[...] = a3
    b_scr[...] = b3

    h_last = carry_ref[...]
    for j in range(t // SUBLANES):
        hj = a_scr[j] * h_last + b_scr[j]
        h_scr[j] = hj
        h_last = hj[SUBLANES - 1:SUBLANES, :]
    carry_ref[...] = h_last

    h = h_scr[...].reshape(t, c)
    o_ref[...] = (h * gate_ref[...].astype(F32)).astype(o_ref.dtype)


def _lru_branch(u, gate, cw, cb, w_a, b_a, w_x, b_x, lam, t=256):
    s, c = u.shape
    t = _tile(s, t)
    nblk, bd, _ = w_a.shape
    row = pl.BlockSpec((1, c), lambda i: (0, 0))
    wblk = pl.BlockSpec((nblk, bd, bd), lambda i: (0, 0, 0))
    cw_pad = jnp.pad(cw, ((0, SUBLANES - LRU_CONV_WIDTH), (0, 0)))
    return pl.pallas_call(
        functools.partial(_lru_kernel, t=t),
        out_shape=jax.ShapeDtypeStruct((s, c), BF16),
        grid=(s // t,),
        in_specs=[pl.BlockSpec((t, c), lambda i: (i, 0)),
                  pl.BlockSpec((SUBLANES, c), lambda i: (jnp.maximum(i * (t // SUBLANES) - 1, 0), 0)),
                  pl.BlockSpec((t, c), lambda i: (i, 0)),
                  pl.BlockSpec((SUBLANES, c), lambda i: (0, 0)),
                  row, wblk, row, wblk, row, row],
        out_specs=pl.BlockSpec((t, c), lambda i: (i, 0)),
        scratch_shapes=[pltpu.VMEM((t + SUBLANES, c), F32),
                        pltpu.VMEM((t // SUBLANES, SUBLANES, c), F32),
                        pltpu.VMEM((t // SUBLANES, SUBLANES, c), F32),
                        pltpu.VMEM((t // SUBLANES, SUBLANES, c), F32),
                        pltpu.VMEM((1, c), F32)],
        compiler_params=_cparams("arbitrary"),
        name="lru_branch",
    )(u, u, gate, cw_pad, cb.reshape(1, c), w_a, b_a.reshape(1, c), w_x, b_x.reshape(1, c),
      jax.nn.softplus(-lam).reshape(1, c))


HEAD_GROUP = 8
QK_PAD = 2 * LANES


def _q_kernel(a_ref, wn_ref, wr_ref, cos_ref, sin_ref, o_ref, *, scale):
    a = a_ref[...]
    qn = jnp.dot(a, wn_ref[...], preferred_element_type=F32) * scale
    qr = jnp.dot(a, wr_ref[...], preferred_element_type=F32)
    cos, sin = cos_ref[...], sin_ref[...]
    lane = lax.broadcasted_iota(jnp.int32, cos.shape, 1)
    for pair in range(HEAD_GROUP // 2):
        roped = _rope_lanes(qr[:, pair * LANES:(pair + 1) * LANES], cos, sin) * scale
        halves = (jnp.where(lane < QK_ROPE_DIM, roped, 0.0), jnp.where(lane >= QK_ROPE_DIM, roped, 0.0))
        for sub in range(2):
            h = 2 * pair + sub
            o_ref[h, :, 0:QK_NOPE_DIM] = qn[:, h * QK_NOPE_DIM:(h + 1) * QK_NOPE_DIM].astype(o_ref.dtype)
            o_ref[h, :, QK_NOPE_DIM:QK_PAD] = halves[sub].astype(o_ref.dtype)


def _q_proj(cqn, w_qn, w_qr, cos, sin, scale, tm=1024):
    m, k = cqn.shape
    nh = w_qn.shape[1] // QK_NOPE_DIM
    assert nh % HEAD_GROUP == 0
    tm = _tile(m, tm)
    return pl.pallas_call(
        functools.partial(_q_kernel, scale=scale),
        out_shape=jax.ShapeDtypeStruct((nh, m, QK_PAD), BF16),
        grid=(m // tm, nh // HEAD_GROUP),
        in_specs=[pl.BlockSpec((tm, k), lambda i, g: (i, 0)),
                  pl.BlockSpec((k, HEAD_GROUP * QK_NOPE_DIM), lambda i, g: (0, g)),
                  pl.BlockSpec((k, HEAD_GROUP * QK_ROPE_DIM), lambda i, g: (0, g)),
                  pl.BlockSpec((tm, LANES), lambda i, g: (i, 0)),
                  pl.BlockSpec((tm, LANES), lambda i, g: (i, 0))],
        out_specs=pl.BlockSpec((HEAD_GROUP, tm, QK_PAD), lambda i, g: (g, i, 0)),
        compiler_params=_cparams("parallel", "parallel"),
        name="q_proj",
    )(cqn, w_qn, w_qr, cos, sin)


def _kv_kernel(a_ref, wk_ref, wv_ref, kr_ref, k_ref, v_ref):
    a = a_ref[...]
    kn = jnp.dot(a, wk_ref[...], preferred_element_type=F32)
    v = jnp.dot(a, wv_ref[...], preferred_element_type=F32)
    kr = kr_ref[...]
    for h in range(HEAD_GROUP):
        k_ref[h, :, 0:QK_NOPE_DIM] = kn[:, h * QK_NOPE_DIM:(h + 1) * QK_NOPE_DIM].astype(k_ref.dtype)
        k_ref[h, :, QK_NOPE_DIM:QK_PAD] = kr
        v_ref[h] = v[:, h * V_DIM:(h + 1) * V_DIM].astype(v_ref.dtype)


def _kv_proj(ckvn, w_kn, w_v, kr, tm=1024):
    m, k = ckvn.shape
    nh = w_kn.shape[1] // QK_NOPE_DIM
    tm = _tile(m, tm)
    return pl.pallas_call(
        _kv_kernel,
        out_shape=(jax.ShapeDtypeStruct((nh, m, QK_PAD), BF16), jax.ShapeDtypeStruct((nh, m, V_DIM), BF16)),
        grid=(m // tm, nh // HEAD_GROUP),
        in_specs=[pl.BlockSpec((tm, k), lambda i, g: (i, 0)),
                  pl.BlockSpec((k, HEAD_GROUP * QK_NOPE_DIM), lambda i, g: (0, g)),
                  pl.BlockSpec((k, HEAD_GROUP * V_DIM), lambda i, g: (0, g)),
                  pl.BlockSpec((tm, LANES), lambda i, g: (i, 0))],
        out_specs=(pl.BlockSpec((HEAD_GROUP, tm, QK_PAD), lambda i, g: (g, i, 0)),
                   pl.BlockSpec((HEAD_GROUP, tm, V_DIM), lambda i, g: (g, i, 0))),
        compiler_params=_cparams("parallel", "parallel"),
        name="kv_proj",
    )(ckvn, w_kn, w_v, kr)


FLASH_HEADS = 2
FLASH_TQ = 1024
FLASH_TK = 1024
FLASH_TD = 256


def _flash_kernel(q_ref, k_ref, v_ref, g_ref, o_ref, m_scr, l_scr, acc_scr, *, tq, tk, td):
    qi = pl.program_id(1)
    nheads = q_ref.shape[0]
    m_scr[...] = jnp.full_like(m_scr, -jnp.inf)
    l_scr[...] = jnp.zeros_like(l_scr)
    acc_scr[...] = jnp.zeros_like(acc_scr)

    def step(off, klen, r0, diagonal):
        rows = slice(r0, tq)
        for h in range(nheads):
            q = q_ref[h, rows, :]
            k = k_ref[h, pl.ds(off, klen), :]
            v = v_ref[h, pl.ds(off, klen), :]
            s = lax.dot_general(q, k, (((1,), (1,)), ((), ())), preferred_element_type=F32)
            if diagonal:
                row = lax.broadcasted_iota(jnp.int32, s.shape, 0)
                col = lax.broadcasted_iota(jnp.int32, s.shape, 1)
                s = jnp.where(col <= row, s, jnp.finfo(F32).min)
            m_prev = m_scr[h, rows, :]
            m_new = jnp.maximum(m_prev, jnp.max(s, axis=1, keepdims=True))
            alpha = jnp.exp2(m_prev - m_new)
            p = jnp.exp2(s - jnp.tile(m_new, (1, klen // LANES)))
            psum = p[:, 0:LANES]
            for c in range(1, klen // LANES):
                psum = psum + p[:, c * LANES:(c + 1) * LANES]
            l_scr[h, rows, :] = alpha * l_scr[h, rows, :] + psum
            acc_scr[h, rows, :] = alpha * acc_scr[h, rows, :] + jnp.dot(
                p.astype(v.dtype), v, preferred_element_type=F32)
            m_scr[h, rows, :] = m_new

    def body(ki, carry):
        step(pl.multiple_of(ki * tk, tk), tk, 0, False)
        return carry

    n_full = qi * (tq // tk)

    def body2(ki, carry):
        step(pl.multiple_of(2 * ki * tk, tk), tk, 0, False)
        step(pl.multiple_of((2 * ki + 1) * tk, tk), tk, 0, False)
        return carry

    lax.fori_loop(0, n_full // 2, body2, 0)
    lax.fori_loop(n_full - n_full % 2, n_full, body, 0)
    q0 = pl.multiple_of(qi * tq, tq)
    for r in range(tq // td):
        step(q0 + r * td, td, r * td, True)
    for h in range(nheads):
        cols = slice(h * V_DIM, (h + 1) * V_DIM)
        o = acc_scr[h] / jnp.sum(l_scr[h], axis=1, keepdims=True)
        o_ref[:, cols] = (o * g_ref[:, cols].astype(F32)).astype(o_ref.dtype)


def _flash(q, k, v, gate):
    nh, s, _ = q.shape
    hp = FLASH_HEADS
    tq = _tile(s, FLASH_TQ)
    tk = _tile(tq, FLASH_TK)
    td = _tile(tk, FLASH_TD)
    assert nh % hp == 0
    return pl.pallas_call(
        functools.partial(_flash_kernel, tq=tq, tk=tk, td=td),
        out_shape=jax.ShapeDtypeStruct((s, nh * V_DIM), BF16),
        grid=(nh // hp, s // tq),
        in_specs=[pl.BlockSpec((hp, tq, QK_PAD), lambda h, i: (h, i, 0)),
                  pl.BlockSpec((hp, s, QK_PAD), lambda h, i: (h, 0, 0)),
                  pl.BlockSpec((hp, s, V_DIM), lambda h, i: (h, 0, 0)),
                  pl.BlockSpec((tq, hp * V_DIM), lambda h, i: (i, h))],
        out_specs=pl.BlockSpec((tq, hp * V_DIM), lambda h, i: (i, h)),
        scratch_shapes=[pltpu.VMEM((hp, tq, LANES), F32), pltpu.VMEM((hp, tq, LANES), F32),
                        pltpu.VMEM((hp, tq, V_DIM), F32)],
        compiler_params=_cparams("parallel", "parallel"),
        name="flash",
    )(q, k, v, gate)


def _merge_kernel(ya_ref, yb_ref, yc_ref, wa_ref, wb_ref, wc_ref, g0_ref, g1_ref, g2_ref, o_ref):
    pa = jnp.dot(ya_ref[...], wa_ref[...], preferred_element_type=F32)
    pb = jnp.dot(yb_ref[...], wb_ref[...], preferred_element_type=F32)
    pc = jnp.dot(yc_ref[...], wc_ref[...], preferred_element_type=F32)
    merged = (g0_ref[...].astype(F32) * pa + g1_ref[...].astype(F32) * pb) + g2_ref[...].astype(F32) * pc
    o_ref[...] = merged.astype(o_ref.dtype)


def _merge(ya, yb, yc, wa, wb, wc, gm, tm=512, tn=512):
    m = ya.shape[0]
    n = wa.shape[1]
    tm, tn = _tile(m, tm), _tile(n, tn)
    nb = n // tn
    lhs = lambda y: pl.BlockSpec((tm, y.shape[1]), lambda i, j: (i, 0))
    rhs = lambda w: pl.BlockSpec((w.shape[0], tn), lambda i, j: (0, j))
    gate = lambda b: pl.BlockSpec((tm, tn), lambda i, j: (i, j + b * nb))
    return pl.pallas_call(
        _merge_kernel,
        out_shape=jax.ShapeDtypeStruct((m, n), BF16),
        grid=(m // tm, nb),
        in_specs=[lhs(ya), lhs(yb), lhs(yc), rhs(wa), rhs(wb), rhs(wc), gate(0), gate(1), gate(2)],
        out_specs=pl.BlockSpec((tm, tn), lambda i, j: (i, j)),
        compiler_params=_cparams("parallel", "parallel"),
        name="merge",
    )(ya, yb, yc, wa, wb, wc, gm, gm, gm)


def kernel(x, positions, pre_norm_g, w_in, conv_dw_w, conv_dw_b, conv_ln_g, conv_ln_b, w_conv_proj, q_norm_g, w_uq, kv_norm_g, w_ukv, w_mla_proj, lru_conv_w, lru_conv_b, lru_w_a, lru_b_a, lru_w_x, lru_b_x, lru_lambda, w_lru_proj, w_out, post_norm_g):
    batch, seq, d_model = x.shape
    assert batch == 1
    depth = w_in.shape[0]
    nh = N_HEADS
    d_conv = conv_dw_w.shape[2]
    d_lru = lru_conv_w.shape[2]
    q_lora = w_uq.shape[1]
    kv_lora = w_ukv.shape[1]
    d_mla = nh * V_DIM
    qk_dim = QK_NOPE_DIM + QK_ROPE_DIM
    scale = qk_dim ** -0.5 * float(np.log2(np.e))
    splits = np.cumsum([0, 2 * d_conv, d_conv, q_lora, kv_lora, QK_ROPE_DIM, d_mla, d_lru, d_lru, 3 * d_model])
    (c_glu, c_gconv, c_cq, c_ckv, c_kr, c_gmla, c_ulru, c_glru, c_gm, c_end) = [int(v) for v in splits]
    assert c_end == w_in.shape[2]

    cos, sin = _rope_tables(positions[0])
    xs = x[0]
    h = _prenorm(xs, pre_norm_g[0])

    w_in_nk = jnp.swapaxes(w_in, 1, 2)
    inproj = functools.partial(_proj, w_is_nk=True)

    for l in range(depth):
        cast = lambda w: w.astype(BF16)
        w_lat = lax.optimization_barrier(w_in_nk[l, c_cq:c_gmla])
        w_cq = cast(w_lat[:q_lora])
        w_ckvr = cast(jnp.concatenate([w_lat[q_lora:], w_lat[q_lora + kv_lora:]], axis=0))
        uq = w_uq[l].reshape(q_lora, nh, qk_dim)
        w_qn = cast(uq[:, :, :QK_NOPE_DIM].reshape(q_lora, nh * QK_NOPE_DIM))
        w_qr = cast(uq[:, :, QK_NOPE_DIM:].reshape(q_lora, nh * QK_ROPE_DIM))
        ukv = w_ukv[l].reshape(kv_lora, nh, QK_NOPE_DIM + V_DIM)
        w_kn = cast(ukv[:, :, :QK_NOPE_DIM].reshape(kv_lora, nh * QK_NOPE_DIM))
        w_v = cast(ukv[:, :, QK_NOPE_DIM:].reshape(kv_lora, nh * V_DIM))

        y_glu = inproj(h, w_in_nk, l, c_glu, d_conv, F32, act="glu", tn=256, name="glu_proj")
        cqn = _cq_proj(h, w_cq, q_norm_g[l])
        ckvn, kr = _ckvr_proj(h, w_ckvr, kv_norm_g[l], cos, sin)
        u_lru = inproj(h, w_in_nk, l, c_ulru, d_lru, F32, name="ulru_proj")
        g_conv = inproj(h, w_in_nk, l, c_gconv, d_conv, BF16, act="silu", name="conv_gate_proj")
        g_mla = inproj(h, w_in_nk, l, c_gmla, d_mla, BF16, act="silu", name="mla_gate_proj")
        g_lru = inproj(h, w_in_nk, l, c_glru, d_lru, BF16, act="silu", name="lru_gate_proj")
        gm = inproj(h, w_in_nk, l, c_gm, 3 * d_model, BF16, act="sigmoid", name="merge_gate_proj")

        y_a = _conv_branch(y_glu, g_conv, conv_dw_w[l], conv_dw_b[l], conv_ln_g[l], conv_ln_b[l])
        q = _q_proj(cqn, w_qn, w_qr, cos, sin, scale)
        k, v = _kv_proj(ckvn, w_kn, w_v, kr)
        y_b = _flash(q, k, v, g_mla)
        y_c = _lru_branch(u_lru, g_lru, lru_conv_w[l], lru_conv_b[l], cast(lru_w_a[l]),
                          lru_b_a[l], cast(lru_w_x[l]), lru_b_x[l], lru_lambda[l])

        merged = _merge(y_a, y_b, y_c, cast(w_conv_proj[l]), cast(w_mla_proj[l]), cast(w_lru_proj[l]), gm)
        out = _proj(merged, w_out, l, 0, d_model, F32, name="out_proj")
        g_next = pre_norm_g[l + 1] if l + 1 < depth else pre_norm_g[l]
        xs, h = _post(xs, out, post_norm_g[l], g_next)
    return xs[None]
```
